```python
import math
import jax, jax.numpy as jnp
from jax import lax
import numpy as np

D_MODEL = 1024
BATCH = 2
SEQ = 16384
DEPTH = 1
DEC_BATCH = 16
DEC_SEQ = 4096
PAST_LEN = 128

HEAD_DIM = 64
ATT_HEADS = 4
ATT_V = 2 * HEAD_DIM
ATT_QK_WIDTH = ATT_HEADS * 2 * HEAD_DIM
ATT_WIDTH = ATT_HEADS * ATT_V
ROT_DIM = HEAD_DIM // 4
ROPE_THETA = 500000.0
Q_BLOCK = 128
HG_HEADS = 4
HG_K = 128
HG_V = 128
HG_WIDTH = HG_HEADS * HG_V
CHUNK = 16
IN_COLS = 2 * ATT_QK_WIDTH + ATT_WIDTH + 5 * HG_WIDTH
SPLITS = [512, 1024, 1536, 2048, 2560, 3072, 3584]
MIX_WIDTH = ATT_WIDTH + HG_WIDTH
N_EXPERTS = 32
TOP_K = 4
D_FF = 1024
MOE_BLOCK = 512
SWIGLU_LIMIT = 7.0
SWIGLU_ALPHA = 1.702
EPS = 1e-5

kernel_name = "hymba_diffattn_hgrn2_moe_encoder"


def rmsnorm(x, w):
    xf = x.astype(jnp.float32)
    y = xf * lax.rsqrt(jnp.mean(xf * xf, axis=-1, keepdims=True) + EPS)
    return (y * w.astype(jnp.float32)).astype(x.dtype)


def rope_tables(S):
    inv = ROPE_THETA ** (-jnp.arange(0, ROT_DIM, 2, dtype=jnp.float32) / ROT_DIM)
    ang = jnp.arange(S, dtype=jnp.float32)[:, None] * inv[None, :]
    return jnp.cos(ang), jnp.sin(ang)


def apply_partial_rope(x, cos, sin):
    half = ROT_DIM // 2
    c = cos[None, :, None, None, :].astype(x.dtype)
    s = sin[None, :, None, None, :].astype(x.dtype)
    x1, x2, xp = x[..., :half], x[..., half:ROT_DIM], x[..., ROT_DIM:]
    return jnp.concatenate([x1 * c - x2 * s, x2 * c + x1 * s, xp], axis=-1)


def diff_attention(q, k, v, lam):
    B, S = q.shape[0], q.shape[1]
    nblk = S // Q_BLOCK
    qb = (q * (HEAD_DIM ** -0.5)).reshape(B, nblk, Q_BLOCK, ATT_HEADS, 2, HEAD_DIM).swapaxes(0, 1)

    def one_block(qblk):
        s = jnp.einsum('bqhcd,bkhcd->bhcqk', qblk, k).astype(jnp.float32)
        p = jax.nn.softmax(s, axis=-1)
        a = p[:, :, 0] - lam * p[:, :, 1]
        return jnp.einsum('bhqk,bkhv->bqhv', a.astype(v.dtype), v)

    o = lax.map(one_block, qb)
    return o.swapaxes(0, 1).reshape(B, S, ATT_HEADS, ATT_V)


def lower_bound(lb_params, l):
    return jnp.cumsum(jax.nn.softmax(lb_params.astype(jnp.float32), axis=0), axis=0)[l]


def log_forget(f_pre, lb):
    lb = lb.reshape(HG_HEADS, HG_K)
    return jnp.logaddexp(jnp.log(lb), jnp.log1p(-lb) + jax.nn.log_sigmoid(f_pre.astype(jnp.float32)))


def hgrn2_scan(q, lf, v):
    B, S, H, K = q.shape
    V = v.shape[-1]
    n = S // CHUNK

    def to_chunks(a):
        return a.reshape(B, n, CHUNK, H, a.shape[-1]).transpose(1, 0, 3, 2, 4)

    mask = jnp.tril(jnp.ones((CHUNK, CHUNK), dtype=bool))[:, :, None]

    def step(state, inp):
        qb, lfb, vb = inp
        b = jnp.cumsum(lfb, axis=2)
        kb = -jnp.expm1(lfb)
        o_inter = jnp.einsum('bhck,bhkv->bhcv', qb * jnp.exp(b), state)
        diff = b[:, :, :, None, :] - b[:, :, None, :, :]
        decay = jnp.where(mask, jnp.exp(jnp.where(mask, diff, 0.0)), 0.0)
        A = jnp.einsum('bhtk,bhtsk,bhsk->bhts', qb, decay, kb)
        o_intra = jnp.einsum('bhts,bhsv->bhtv', A, vb)
        bl = b[:, :, -1:, :]
        new_state = jnp.exp(bl)[:, :, 0, :, None] * state + \
            jnp.einsum('bhsk,bhsv->bhkv', kb * jnp.exp(bl - b), vb)
        return new_state, o_inter + o_intra

    s0 = jnp.zeros((B, H, K, V), jnp.float32)
    _, o = lax.scan(step, s0, (to_chunks(q), to_chunks(lf), to_chunks(v)))
    return o.transpose(1, 0, 3, 2, 4).reshape(B, S, H, V)


def moe(h, w_router, b_router, w_gate_up, b_gate_up, w_down, b_down):
    T, D = h.shape
    A = T * TOP_K
    nblk = -(-A // MOE_BLOCK) + N_EXPERTS
    P = nblk * MOE_BLOCK
    logits = h.astype(jnp.float32) @ w_router.astype(jnp.float32) + b_router.astype(jnp.float32)
    top_v, top_i = lax.top_k(logits, TOP_K)
    gates = jax.nn.softmax(top_v, axis=-1)
    flat_e = top_i.reshape(-1)
    order = jnp.argsort(flat_e)
    e_sorted = flat_e[order]
    tok_sorted = (order // TOP_K).astype(jnp.int32)
    g_sorted = gates.reshape(-1)[order]
    sizes = jnp.bincount(flat_e, length=N_EXPERTS).astype(jnp.int32)
    padded = (sizes + MOE_BLOCK - 1) // MOE_BLOCK * MOE_BLOCK
    gstart = jnp.cumsum(sizes) - sizes
    pends = jnp.cumsum(padded)
    pstart = pends - padded
    dest = pstart[e_sorted] + jnp.arange(A, dtype=jnp.int32) - gstart[e_sorted]
    row_tok = jnp.full((P,), T, jnp.int32).at[dest].set(tok_sorted)
    row_gate = jnp.zeros((P,), h.dtype).at[dest].set(g_sorted.astype(h.dtype))
    blk_e = jnp.minimum(jnp.searchsorted(pends, jnp.arange(nblk, dtype=jnp.int32) * MOE_BLOCK,
                                         side='right'), N_EXPERTS - 1).astype(jnp.int32)
    h_pad = jnp.concatenate([h, jnp.zeros((1, D), h.dtype)], axis=0)
    xs = h_pad[row_tok].reshape(nblk, MOE_BLOCK, D)

    def expert_block(args):
        xb, e = args
        gu = xb @ w_gate_up[e] + b_gate_up[e]
        glu = jnp.minimum(gu[:, 0::2], SWIGLU_LIMIT)
        lin = jnp.clip(gu[:, 1::2], -SWIGLU_LIMIT, SWIGLU_LIMIT)
        act = glu * jax.nn.sigmoid(SWIGLU_ALPHA * glu) * (lin + 1)
        return act @ w_down[e] + b_down[e]

    y = lax.map(expert_block, (xs, blk_e)).reshape(P, D)
    y = y * row_gate[:, None]
    return jax.ops.segment_sum(y, row_tok, num_segments=T + 1)[:T]


def encoder_layer(x, l, cos, sin, norm_mix, w_in, lambda_q1, lambda_k1, lambda_q2, lambda_k2,
                  subln_w, lb_fwd, lb_bwd, hgrn_norm, w_out, norm_ffn, w_router, b_router,
                  w_gate_up, b_gate_up, w_down, b_down):
    B, S, _ = x.shape
    h = rmsnorm(x, norm_mix[l])
    proj = h @ w_in[l]
    qa, ka, va, qh, ff, fb, ih, gh = jnp.split(proj, SPLITS, axis=-1)

    qa = apply_partial_rope(qa.reshape(B, S, ATT_HEADS, 2, HEAD_DIM), cos, sin)
    ka = apply_partial_rope(ka.reshape(B, S, ATT_HEADS, 2, HEAD_DIM), cos, sin)
    va = va.reshape(B, S, ATT_HEADS, ATT_V)
    lam_init = 0.8 - 0.6 * math.exp(-0.3 * l)
    lam = (jnp.exp(jnp.sum(lambda_q1[l].astype(jnp.float32) * lambda_k1[l].astype(jnp.float32)))
           - jnp.exp(jnp.sum(lambda_q2[l].astype(jnp.float32) * lambda_k2[l].astype(jnp.float32)))
           + lam_init)
    oa = diff_attention(qa, ka, va, lam)
    oa = (rmsnorm(oa, subln_w[l]) * (1.0 - lam_init)).reshape(B, S, ATT_WIDTH)

    qh = qh.reshape(B, S, HG_HEADS, HG_K).astype(jnp.float32)
    vh = ih.reshape(B, S, HG_HEADS, HG_V).astype(jnp.float32)
    lf_f = log_forget(ff.reshape(B, S, HG_HEADS, HG_K), lower_bound(lb_fwd, l))
    lf_b = log_forget(fb.reshape(B, S, HG_HEADS, HG_K), lower_bound(lb_bwd, l))
    o_f = hgrn2_scan(qh, lf_f, vh)
    o_b = jnp.flip(hgrn2_scan(jnp.flip(qh, 1), jnp.flip(lf_b, 1), jnp.flip(vh, 1)), 1)
    oh = (o_f + o_b).reshape(B, S, HG_WIDTH).astype(x.dtype)
    oh = rmsnorm(oh, hgrn_norm[l]) * jax.nn.silu(gh)

    x = x + jnp.concatenate([oa, oh], axis=-1) @ w_out[l]

    h2 = rmsnorm(x, norm_ffn[l]).reshape(B * S, D_MODEL)
    moe_out = moe(h2, w_router[l], b_router[l], w_gate_up[l], b_gate_up[l], w_down[l], b_down[l])
    return x + moe_out.reshape(B, S, D_MODEL)


def setup_inputs(seed: int = 0) -> dict:
    key = jax.random.key(seed)
    ks = jax.random.split(key, 24)
    f32 = jnp.float32
    nrm = lambda k, shape, s: jax.random.normal(k, shape, f32) * s
    return {
        "x_prompt": nrm(ks[0], (BATCH, SEQ, D_MODEL), 1.0),
        "x_sample": nrm(ks[1], (DEC_BATCH, DEC_SEQ, D_MODEL), 1.0),
        "norm_mix": 1.0 + nrm(ks[2], (DEPTH, D_MODEL), 0.02),
        "w_in": nrm(ks[3], (DEPTH, D_MODEL, IN_COLS), D_MODEL ** -0.5),
        "lambda_q1": nrm(ks[4], (DEPTH, HEAD_DIM), 0.1),
        "lambda_k1": nrm(ks[5], (DEPTH, HEAD_DIM), 0.1),
        "lambda_q2": nrm(ks[6], (DEPTH, HEAD_DIM), 0.1),
        "lambda_k2": nrm(ks[7], (DEPTH, HEAD_DIM), 0.1),
        "subln_w": 1.0 + nrm(ks[8], (DEPTH, ATT_V), 0.02),
        "lb_fwd": nrm(ks[9], (DEPTH + 1, HG_HEADS * HG_K), 1.0),
        "lb_bwd": nrm(ks[10], (DEPTH + 1, HG_HEADS * HG_K), 1.0),
        "hgrn_norm": 1.0 + nrm(ks[11], (DEPTH, HG_WIDTH), 0.02),
        "w_out": nrm(ks[12], (DEPTH, MIX_WIDTH, D_MODEL), MIX_WIDTH ** -0.5),
        "norm_ffn": 1.0 + nrm(ks[13], (DEPTH, D_MODEL), 0.02),
        "w_router": nrm(ks[14], (DEPTH, D_MODEL, N_EXPERTS), D_MODEL ** -0.5),
        "b_router": nrm(ks[15], (DEPTH, N_EXPERTS), 0.01),
        "w_gate_up": nrm(ks[16], (DEPTH, N_EXPERTS, D_MODEL, 2 * D_FF), D_MODEL ** -0.5),
        "b_gate_up": nrm(ks[17], (DEPTH, N_EXPERTS, 2 * D_FF), 0.02),
        "w_down": nrm(ks[18], (DEPTH, N_EXPERTS, D_FF, D_MODEL), D_FF ** -0.5),
        "b_down": nrm(ks[19], (DEPTH, N_EXPERTS, D_MODEL), 0.02),
        "norm_final": 1.0 + nrm(ks[20], (D_MODEL,), 0.02),
    }


def reference(x_prompt, x_sample, norm_mix, w_in, lambda_q1, lambda_k1, lambda_q2, lambda_k2,
              subln_w, lb_fwd, lb_bwd, hgrn_norm, w_out, norm_ffn, w_router, b_router,
              w_gate_up, b_gate_up, w_down, b_down, norm_final):
    def trunk(x):
        cos, sin = rope_tables(x.shape[1])
        for l in range(DEPTH):
            x = encoder_layer(x, l, cos, sin, norm_mix, w_in, lambda_q1, lambda_k1, lambda_q2,
                              lambda_k2, subln_w, lb_fwd, lb_bwd, hgrn_norm, w_out, norm_ffn,
                              w_router, b_router, w_gate_up, b_gate_up, w_down, b_down)
        return rmsnorm(x, norm_final)

    y_prompt = trunk(x_prompt)
    y_sample = trunk(x_sample)
    return (y_prompt, y_sample)
```

```python
import functools
import math

import jax
import jax.numpy as jnp
from jax import lax
from jax.experimental import pallas as pl
from jax.experimental.pallas import tpu as pltpu

D_MODEL = 1024
HEAD_DIM = 64
ATT_HEADS = 4
ATT_V = 2 * HEAD_DIM
ATT_WIDTH = ATT_HEADS * ATT_V
ROT_DIM = HEAD_DIM // 4
ROPE_THETA = 500000.0
HG_HEADS = 4
HG_K = 128
HG_WIDTH = HG_HEADS * HG_K
IN_COLS = 4096
GROUP_COLS = 512
N_EXPERTS = 32
TOP_K = 4
D_FF = 1024
SWIGLU_LIMIT = 7.0
SWIGLU_ALPHA = 1.702
EPS = 1e-5
LAM_INIT = 0.8 - 0.6 * math.exp(-0.3 * 0)

LANES = 128
HG_CHUNK = 128
MOE_ROWS = 512
VMEM_LIMIT = 56 * 1024 * 1024

F32 = jnp.float32
BF16 = jnp.bfloat16
NT_DIMS = (((1,), (1,)), ((), ()))
TN_DIMS = (((0,), (0,)), ((), ()))


def _params(*sem):
    return pltpu.CompilerParams(dimension_semantics=sem, vmem_limit_bytes=VMEM_LIMIT)


def _inproj_kernel(x_ref, nw_ref, w_ref, cos_ref, sa_ref, sb_ref, lbf_ref, lbb_ref,
                   q_ref, k_ref, v_ref, qh_ref, lff_ref, lfb_ref, ih_ref, g_ref):
    x = x_ref[...]
    ms = jnp.mean(x * x, axis=-1, keepdims=True)
    h = (x * lax.rsqrt(ms + EPS) * nw_ref[...]).astype(BF16)

    def proj(c):
        return jnp.dot(h, w_ref[:, c * GROUP_COLS:(c + 1) * GROUP_COLS],
                       preferred_element_type=F32)

    cos, sa, sb = cos_ref[...], sa_ref[...], sb_ref[...]

    def rope(p):
        outs = []
        for j in range(GROUP_COLS // LANES):
            xx = p[:, j * LANES:(j + 1) * LANES]
            outs.append(xx * cos + pltpu.roll(xx, LANES - ROT_DIM // 2, 1) * sa
                        + pltpu.roll(xx, ROT_DIM // 2, 1) * sb)
        return jnp.concatenate(outs, axis=1)

    def log_forget(p, lb):
        return jnp.log(lb + (1.0 - lb) * jax.nn.sigmoid(p))

    q_ref[...] = (rope(proj(0)) * (HEAD_DIM ** -0.5)).astype(BF16)
    k_ref[...] = rope(proj(1)).astype(BF16)
    v_ref[...] = proj(2).astype(BF16)
    qh_ref[...] = proj(3).astype(BF16)
    lff_ref[...] = log_forget(proj(4), lbf_ref[...])
    lfb_ref[...] = log_forget(proj(5), lbb_ref[...])
    ih_ref[...] = proj(6).astype(BF16)
    gh = proj(7)
    g_ref[...] = (gh * jax.nn.sigmoid(gh)).astype(BF16)


def _inproj(x2d, seq, norm_w, w_in, cos_t, sa_t, sb_t, lbf, lbb):
    T = x2d.shape[0]
    tm = min(512, seq)
    nseq = seq // tm
    row = lambda i: (i, 0)
    fixed = lambda i: (0, 0)
    tab = lambda i: (i % nseq, 0)
    wide = lambda dt: jax.ShapeDtypeStruct((T, GROUP_COLS), dt)
    out_dt = [BF16, BF16, BF16, BF16, F32, F32, BF16, BF16]
    return pl.pallas_call(
        _inproj_kernel,
        grid=(T // tm,),
        in_specs=[
            pl.BlockSpec((tm, D_MODEL), row),
            pl.BlockSpec((1, D_MODEL), fixed),
            pl.BlockSpec((D_MODEL, IN_COLS), fixed),
            pl.BlockSpec((tm, LANES), tab),
            pl.BlockSpec((tm, LANES), tab),
            pl.BlockSpec((tm, LANES), tab),
            pl.BlockSpec((1, GROUP_COLS), fixed),
            pl.BlockSpec((1, GROUP_COLS), fixed),
        ],
        out_specs=[pl.BlockSpec((tm, GROUP_COLS), row) for _ in out_dt],
        out_shape=[wide(dt) for dt in out_dt],
        compiler_params=_params("parallel"),
        name="inproj",
    )(x2d, norm_w, w_in, cos_t, sa_t, sb_t, lbf, lbb)


def _attn_kernel(lam_ref, q_ref, k_ref, v_ref, w_ref, o_ref, q1_s, q2_s, m_s, l_s, acc_s):
    j = pl.program_id(3)
    tq, tk = q_ref.shape[0], k_ref.shape[0]

    @pl.when(j == 0)
    def _():
        q = q_ref[...]
        lane = lax.broadcasted_iota(jnp.int32, q.shape, 1)
        zero = jnp.zeros_like(q)
        q1_s[...] = jnp.where(lane < HEAD_DIM, q, zero)
        q2_s[...] = jnp.where(lane >= HEAD_DIM, q, zero)
        m_s[...] = jnp.full(m_s.shape, -jnp.inf, F32)
        l_s[...] = jnp.zeros(l_s.shape, F32)
        acc_s[...] = jnp.zeros(acc_s.shape, F32)

    k = k_ref[...]
    v = v_ref[...]
    for c, qs in enumerate((q1_s, q2_s)):
        s = lax.dot_general(qs[...], k, NT_DIMS, preferred_element_type=F32)
        m_prev = m_s[c]
        m_next = jnp.maximum(m_prev, jnp.max(s, axis=1, keepdims=True))
        alpha = jnp.exp(m_prev - m_next)
        p = jnp.exp(s - jnp.tile(m_next, (1, tk // LANES)))
        l_s[c] = alpha * l_s[c] + jnp.sum(p, axis=1, keepdims=True)
        acc_s[c] = acc_s[c] * alpha + jnp.dot(p.astype(BF16), v, preferred_element_type=F32)
        m_s[c] = m_next

    @pl.when(j == pl.num_programs(3) - 1)
    def _():
        lam = lam_ref[0]
        o = acc_s[0] / l_s[0] - lam * (acc_s[1] / l_s[1])
        ms = jnp.mean(o * o, axis=-1, keepdims=True)
        o = o * lax.rsqrt(ms + EPS) * w_ref[...] * (1.0 - LAM_INIT)
        o_ref[...] = o.astype(o_ref.dtype)


def _attention(q, k, v, lam, subln_w):
    B, S, _ = q.shape
    tq = min(512, S)
    tk = min(1024, S)
    return pl.pallas_call(
        _attn_kernel,
        grid=(B, ATT_HEADS, S // tq, S // tk),
        in_specs=[
            pl.BlockSpec(memory_space=pltpu.SMEM),
            pl.BlockSpec((None, tq, ATT_V), lambda b, h, i, j: (b, i, h)),
            pl.BlockSpec((None, tk, ATT_V), lambda b, h, i, j: (b, j, h)),
            pl.BlockSpec((None, tk, ATT_V), lambda b, h, i, j: (b, j, h)),
            pl.BlockSpec((1, ATT_V), lambda b, h, i, j: (0, 0)),
        ],
        out_specs=pl.BlockSpec((None, tq, ATT_V), lambda b, h, i, j: (b, i, h)),
        out_shape=jax.ShapeDtypeStruct((B, S, ATT_WIDTH), BF16),
        scratch_shapes=[
            pltpu.VMEM((tq, ATT_V), BF16),
            pltpu.VMEM((tq, ATT_V), BF16),
            pltpu.VMEM((2, tq, LANES), F32),
            pltpu.VMEM((2, tq, LANES), F32),
            pltpu.VMEM((2, tq, ATT_V), F32),
        ],
        compiler_params=_params("parallel", "parallel", "parallel", "arbitrary"),
        name="diff_attention",
    )(lam, q, k, v, subln_w)


def _hgrn_kernel(q_ref, lf_ref, v_ref, lvl_ref, o_ref, st_ref, *, reverse):
    C = q_ref.shape[0]

    @pl.when(pl.program_id(1) == 0)
    def _():
        st_ref[...] = jnp.zeros(st_ref.shape, F32)

    lf = lf_ref[...]
    q = q_ref[...].astype(F32)
    v = v_ref[...]
    kk = 1.0 - jnp.exp(lf)
    row = lax.broadcasted_iota(jnp.int32, lf.shape, 0)
    hi_side, lo_side = (kk, q) if reverse else (q, kk)

    pref, tot = lf, lf
    xs = []
    m = 1
    while m < C:
        upper = (row & m) != 0
        if reverse:
            g = jnp.where(upper, pref - lf, tot - pref + lf)
        else:
            g = jnp.where(upper, pref, tot - pref)
        xs.append((jnp.where(upper, hi_side, lo_side) * jnp.exp(g)).astype(BF16))
        below = pltpu.roll(tot, m, 0)
        above = pltpu.roll(tot, C - m, 0)
        pref = jnp.where(upper, pref + below, pref)
        tot = tot + jnp.where(upper, below, above)
        m *= 2

    if reverse:
        gq, gk = tot - pref + lf, pref - lf
    else:
        gq, gk = pref, tot - pref
    xq = (q * jnp.exp(gq)).astype(BF16)
    xk = (kk * jnp.exp(gk)).astype(BF16)
    chunk_decay = jnp.exp(tot[0:1, :])
    qk = q * kk
    lvl = lvl_ref[...]

    for h in range(HG_HEADS):
        sl = slice(h * HG_K, (h + 1) * HG_K)
        a = jnp.zeros((C, C), F32)
        for li, x in enumerate(xs):
            xh = x[:, sl]
            am = lax.dot_general(xh, xh, NT_DIMS, preferred_element_type=F32)
            a = jnp.where(lvl == li + 1, am, a)
        vh = v[:, sl]
        st = st_ref[h]
        o = jnp.dot(a.astype(BF16), vh, preferred_element_type=F32)
        o += lax.dot_general(xq[:, sl], st.astype(BF16), NT_DIMS, preferred_element_type=F32)
        o += jnp.sum(qk[:, sl], axis=1, keepdims=True) * vh.astype(F32)
        o_ref[:, sl] = o
        st_ref[h] = st * chunk_decay[:, sl] + lax.dot_general(
            vh, xk[:, sl], TN_DIMS, preferred_element_type=F32)


def _level_table(reverse):
    t = jnp.arange(HG_CHUNK, dtype=jnp.int32)[:, None]
    s = jnp.arange(HG_CHUNK, dtype=jnp.int32)[None, :]
    x = t ^ s
    lvl = jnp.zeros_like(x)
    for b in range(HG_CHUNK.bit_length() - 1):
        lvl = jnp.where(x >= (1 << b), b + 1, lvl)
    active = (s > t) if reverse else (s < t)
    return jnp.where(active, lvl, 0)


def _hgrn(qh, lf, vh, reverse):
    B, S, _ = qh.shape
    C = HG_CHUNK
    nc = S // C
    if reverse:
        blk = lambda b, c: (b, nc - 1 - c, 0)
    else:
        blk = lambda b, c: (b, c, 0)
    spec = pl.BlockSpec((None, C, HG_WIDTH), blk)
    return pl.pallas_call(
        functools.partial(_hgrn_kernel, reverse=reverse),
        grid=(B, nc),
        in_specs=[spec, spec, spec, pl.BlockSpec((C, C), lambda b, c: (0, 0))],
        out_specs=spec,
        out_shape=jax.ShapeDtypeStruct((B, S, HG_WIDTH), F32),
        scratch_shapes=[pltpu.VMEM((HG_HEADS, HG_K, HG_K), F32)],
        compiler_params=_params("parallel", "arbitrary"),
        name="hgrn_bwd" if reverse else "hgrn_fwd",
    )(qh, lf, vh, _level_table(reverse))


def _outproj_kernel(x_ref, oa_ref, of_ref, ob_ref, g_ref, hn_ref, wo_ref, fn_ref, wr_ref, br_ref,
                    x1_ref, h2_ref, ti_ref, gate_ref, rank_ref, cnt_ref, run_s):
    tm = x_ref.shape[0]

    @pl.when(pl.program_id(0) == 0)
    def _():
        run_s[...] = jnp.zeros(run_s.shape, F32)

    oh = of_ref[...] + ob_ref[...]
    ms = jnp.mean(oh * oh, axis=-1, keepdims=True)
    ohn = oh * lax.rsqrt(ms + EPS) * hn_ref[...] * g_ref[...].astype(F32)
    mix = jnp.dot(oa_ref[...], wo_ref[:ATT_WIDTH, :], preferred_element_type=F32)
    mix += jnp.dot(ohn.astype(BF16), wo_ref[ATT_WIDTH:, :], preferred_element_type=F32)
    x1 = x_ref[...] + mix
    x1_ref[...] = x1
    ms = jnp.mean(x1 * x1, axis=-1, keepdims=True)
    h2 = x1 * lax.rsqrt(ms + EPS) * fn_ref[...]
    h2_ref[...] = h2

    logits = jnp.dot(h2, wr_ref[...], preferred_element_type=F32,
                     precision=lax.Precision.HIGHEST) + br_ref[...]
    lane = lax.broadcasted_iota(jnp.int32, logits.shape, 1)
    vals, idxs = [], []
    for _ in range(TOP_K):
        mx = jnp.max(logits, axis=1, keepdims=True)
        ix = jnp.min(jnp.where(logits == mx, lane, LANES), axis=1, keepdims=True)
        vals.append(mx)
        idxs.append(ix)
        logits = jnp.where(lane == ix, -jnp.inf, logits)
    es = [jnp.exp(vv - vals[0]) for vv in vals]
    den = es[0] + es[1] + es[2] + es[3]

    onehots = [(lane == ix).astype(F32) for ix in idxs]
    multi = onehots[0] + onehots[1] + onehots[2] + onehots[3]
    r_i = lax.broadcasted_iota(jnp.int32, (tm, tm), 0)
    c_i = lax.broadcasted_iota(jnp.int32, (tm, tm), 1)
    tri = jnp.where(c_i < r_i, 1.0, 0.0).astype(BF16)
    before = jnp.dot(tri, multi.astype(BF16), preferred_element_type=F32) + run_s[...]

    ti = jnp.zeros(logits.shape, jnp.int32)
    gates = jnp.zeros(logits.shape, F32)
    ranks = jnp.zeros(logits.shape, jnp.int32)
    for kslot in range(TOP_K):
        here = lane == kslot
        ti = jnp.where(here, idxs[kslot], ti)
        gates = jnp.where(here, es[kslot] / den, gates)
        rk = jnp.sum(onehots[kslot] * before, axis=1, keepdims=True)
        ranks = jnp.where(here, rk.astype(jnp.int32), ranks)
    ti_ref[...] = ti
    gate_ref[...] = gates
    rank_ref[...] = ranks
    run_s[...] = run_s[...] + jnp.sum(multi, axis=0, keepdims=True)
    cnt_ref[...] = run_s[...]


def _outproj(x2d, oa, o_f, o_b, g, hgrn_norm, w_out, norm_ffn, w_router, b_router):
    T = x2d.shape[0]
    tm = min(256, T)
    row = lambda i: (i, 0)
    fixed = lambda i: (0, 0)
    return pl.pallas_call(
        _outproj_kernel,
        grid=(T // tm,),
        in_specs=[
            pl.BlockSpec((tm, D_MODEL), row),
            pl.BlockSpec((tm, ATT_WIDTH), row),
            pl.BlockSpec((tm, HG_WIDTH), row),
            pl.BlockSpec((tm, HG_WIDTH), row),
            pl.BlockSpec((tm, HG_WIDTH), row),
            pl.BlockSpec((1, HG_WIDTH), fixed),
            pl.BlockSpec((D_MODEL, D_MODEL), fixed),
            pl.BlockSpec((1, D_MODEL), fixed),
            pl.BlockSpec((D_MODEL, LANES), fixed),
            pl.BlockSpec((1, LANES), fixed),
        ],
        out_specs=[
            pl.BlockSpec((tm, D_MODEL), row),
            pl.BlockSpec((tm, D_MODEL), row),
            pl.BlockSpec((tm, LANES), row),
            pl.BlockSpec((tm, LANES), row),
            pl.BlockSpec((tm, LANES), row),
            pl.BlockSpec((1, LANES), fixed),
        ],
        out_shape=[
            jax.ShapeDtypeStruct((T, D_MODEL), F32),
            jax.ShapeDtypeStruct((T, D_MODEL), F32),
            jax.ShapeDtypeStruct((T, LANES), jnp.int32),
            jax.ShapeDtypeStruct((T, LANES), F32),
            jax.ShapeDtypeStruct((T, LANES), jnp.int32),
            jax.ShapeDtypeStruct((1, LANES), F32),
        ],
        scratch_shapes=[pltpu.VMEM((1, LANES), F32)],
        compiler_params=_params("arbitrary"),
        name="outproj_router",
    )(x2d, oa, o_f, o_b, g, hgrn_norm, w_out, norm_ffn, w_router, b_router)


def _row_copy(src, src_row, dst, dst_row, sem):
    return pltpu.make_async_copy(src.at[pl.ds(src_row, 1)], dst.at[pl.ds(dst_row, 1)], sem)


def _dispatch_kernel(pad_start_ref, pad_cnt_ref, nu_ref, dest_ref, h2_ref, xs_ref, zero_s, sem, zsem):
    i = pl.program_id(0)
    n = dest_ref.shape[0]
    tm = n // TOP_K

    def issue(a, carry):
        _row_copy(h2_ref, i * tm + a // TOP_K, xs_ref, dest_ref[a], sem).start()
        return carry

    lax.fori_loop(0, n, issue, 0)

    @pl.when(i == 0)
    def _():
        zero_s[...] = jnp.zeros(zero_s.shape, F32)
        for e in range(N_EXPERTS):
            start, cnt = pad_start_ref[e], pad_cnt_ref[e]

            def fill(r, carry):
                _row_copy(zero_s, 0, xs_ref, start + r, zsem).start()
                return carry

            def drain(r, carry):
                _row_copy(zero_s, 0, xs_ref, start + r, zsem).wait()
                return carry

            lax.fori_loop(0, cnt, fill, 0)
            lax.fori_loop(0, cnt, drain, 0)

        def tail_copy(blk):
            return pltpu.make_async_copy(zero_s, xs_ref.at[pl.ds(blk * MOE_ROWS, MOE_ROWS)], zsem)

        def fill_tail(blk, carry):
            tail_copy(blk).start()
            return carry

        def drain_tail(blk, carry):
            tail_copy(blk).wait()
            return carry

        nblk = xs_ref.shape[0] // MOE_ROWS
        lax.fori_loop(nu_ref[0], nblk, fill_tail, 0)
        lax.fori_loop(nu_ref[0], nblk, drain_tail, 0)

    def drain_rows(a, carry):
        _row_copy(h2_ref, i * tm + a // TOP_K, xs_ref, dest_ref[a], sem).wait()
        return carry

    lax.fori_loop(0, n, drain_rows, 0)


def _dispatch(h2, dest_flat, pad_start, pad_cnt, n_used, n_rows):
    T = h2.shape[0]
    tm = min(256, T)
    return pl.pallas_call(
        _dispatch_kernel,
        grid_spec=pltpu.PrefetchScalarGridSpec(
            num_scalar_prefetch=3,
            grid=(T // tm,),
            in_specs=[
                pl.BlockSpec((tm * TOP_K,), lambda i, ps, pc, nu: (i,), memory_space=pltpu.SMEM),
                pl.BlockSpec(memory_space=pl.ANY),
            ],
            out_specs=pl.BlockSpec(memory_space=pl.ANY),
            scratch_shapes=[
                pltpu.VMEM((MOE_ROWS, D_MODEL), F32),
                pltpu.SemaphoreType.DMA(()),
                pltpu.SemaphoreType.DMA(()),
            ],
        ),
        out_shape=jax.ShapeDtypeStruct((n_rows, D_MODEL), F32),
        compiler_params=_params("arbitrary"),
        name="moe_dispatch",
    )(pad_start, pad_cnt, n_used, dest_flat, h2)


def _expert_kernel(be_ref, nu_ref, xs_ref, wg_ref, wl_ref, bg_ref, bl_ref, wd_ref, bd_ref, y_ref):
    @pl.when(pl.program_id(0) < nu_ref[0])
    def _():
        x = xs_ref[...].astype(BF16)
        glu = jnp.dot(x, wg_ref[...], preferred_element_type=F32) + bg_ref[...]
        lin = jnp.dot(x, wl_ref[...], preferred_element_type=F32) + bl_ref[...]
        glu = jnp.minimum(glu, SWIGLU_LIMIT)
        lin = jnp.clip(lin, -SWIGLU_LIMIT, SWIGLU_LIMIT)
        act = glu * jax.nn.sigmoid(SWIGLU_ALPHA * glu) * (lin + 1.0)
        y_ref[...] = jnp.dot(act.astype(BF16), wd_ref[...], preferred_element_type=F32) + bd_ref[...]

    @pl.when(pl.program_id(0) >= nu_ref[0])
    def _():
        y_ref[...] = jnp.zeros(y_ref.shape, F32)


def _experts(xs, blk_e, n_used, wg, wl, bg, bl, wd, bd):
    n_rows = xs.shape[0]
    nblk = n_rows // MOE_ROWS
    rows = lambda i, be, nu: (jnp.minimum(i, nu[0] - 1), 0)
    per_e = lambda i, be, nu: (be[jnp.minimum(i, nu[0] - 1)], 0, 0)
    return pl.pallas_call(
        _expert_kernel,
        grid_spec=pltpu.PrefetchScalarGridSpec(
            num_scalar_prefetch=2,
            grid=(nblk,),
            in_specs=[
                pl.BlockSpec((MOE_ROWS, D_MODEL), rows),
                pl.BlockSpec((None, D_MODEL, D_FF), per_e),
                pl.BlockSpec((None, D_MODEL, D_FF), per_e),
                pl.BlockSpec((None, 1, D_FF), per_e),
                pl.BlockSpec((None, 1, D_FF), per_e),
                pl.BlockSpec((None, D_FF, D_MODEL), per_e),
                pl.BlockSpec((None, 1, D_MODEL), per_e),
            ],
            out_specs=pl.BlockSpec((MOE_ROWS, D_MODEL), lambda i, be, nu: (i, 0)),
        ),
        out_shape=jax.ShapeDtypeStruct((n_rows, D_MODEL), F32),
        compiler_params=_params("arbitrary"),
        name="moe_experts",
    )(blk_e, n_used, xs, wg, wl, bg, bl, wd, bd)


def _combine_kernel(dest_ref, y_ref, x1_ref, gate_ref, nf_ref, o_ref, buf, sem):
    n = dest_ref.shape[0]

    def issue(a, carry):
        pltpu.make_async_copy(y_ref.at[pl.ds(dest_ref[a], 1)],
                              buf.at[a % TOP_K, pl.ds(a // TOP_K, 1)], sem).start()
        return carry

    def drain(a, carry):
        pltpu.make_async_copy(y_ref.at[pl.ds(dest_ref[a], 1)],
                              buf.at[a % TOP_K, pl.ds(a // TOP_K, 1)], sem).wait()
        return carry

    lax.fori_loop(0, n, issue, 0)
    lax.fori_loop(0, n, drain, 0)

    gates = gate_ref[...]
    acc = x1_ref[...]
    for kslot in range(TOP_K):
        acc = acc + gates[:, kslot:kslot + 1] * buf[kslot]
    ms = jnp.mean(acc * acc, axis=-1, keepdims=True)
    o_ref[...] = acc * lax.rsqrt(ms + EPS) * nf_ref[...]


def _combine(y, dest_flat, x1, gates, norm_final):
    T = x1.shape[0]
    tm = min(256, T)
    row = lambda i: (i, 0)
    return pl.pallas_call(
        _combine_kernel,
        grid=(T // tm,),
        in_specs=[
            pl.BlockSpec((tm * TOP_K,), lambda i: (i,), memory_space=pltpu.SMEM),
            pl.BlockSpec(memory_space=pl.ANY),
            pl.BlockSpec((tm, D_MODEL), row),
            pl.BlockSpec((tm, LANES), row),
            pl.BlockSpec((1, D_MODEL), lambda i: (0, 0)),
        ],
        out_specs=pl.BlockSpec((tm, D_MODEL), row),
        out_shape=jax.ShapeDtypeStruct((T, D_MODEL), F32),
        scratch_shapes=[pltpu.VMEM((TOP_K, tm, D_MODEL), F32), pltpu.SemaphoreType.DMA(())],
        compiler_params=_params("arbitrary"),
        name="moe_combine",
    )(dest_flat, y, x1, gates, norm_final)


def _rope_tables(seq):
    half = ROT_DIM // 2
    inv = ROPE_THETA ** (-jnp.arange(0, ROT_DIM, 2, dtype=F32) / ROT_DIM)
    ang = jnp.arange(seq, dtype=F32)[:, None] * inv[None, :]
    cos, sin = jnp.cos(ang), jnp.sin(ang)
    pad = jnp.zeros((seq, HEAD_DIM - ROT_DIM), F32)
    zero = jnp.zeros((seq, half), F32)
    cos_t = jnp.concatenate([cos, cos, pad + 1.0], axis=1)
    sa_t = jnp.concatenate([-sin, zero, pad], axis=1)
    sb_t = jnp.concatenate([zero, sin, pad], axis=1)
    rep = LANES // HEAD_DIM
    return tuple(jnp.tile(t, (1, rep)) for t in (cos_t, sa_t, sb_t))


def _trunk(x, p):
    B, S, _ = x.shape
    T = B * S
    x2d = x.reshape(T, D_MODEL)
    q, k, v, qh, lff, lfb, ih, g = _inproj(x2d, S, p["norm_mix"], p["w_in"], *_rope_tables(S),
                                           p["lb_fwd"], p["lb_bwd"])
    to3 = lambda a: a.reshape(B, S, a.shape[-1])
    oa = _attention(to3(q), to3(k), to3(v), p["lam"], p["subln_w"])
    o_f = _hgrn(to3(qh), to3(lff), to3(ih), reverse=False)
    o_b = _hgrn(to3(qh), to3(lfb), to3(ih), reverse=True)
    x1, h2, top_i, gates, ranks, counts = _outproj(
        x2d, oa.reshape(T, ATT_WIDTH), o_f.reshape(T, HG_WIDTH), o_b.reshape(T, HG_WIDTH), g,
        p["hgrn_norm"], p["w_out"], p["norm_ffn"], p["w_router"], p["b_router"])

    n_rows = (T * TOP_K // MOE_ROWS + N_EXPERTS) * MOE_ROWS
    sizes = counts[0, :N_EXPERTS].astype(jnp.int32)
    padded = (sizes + MOE_ROWS - 1) // MOE_ROWS * MOE_ROWS
    pends = jnp.cumsum(padded)
    pstart = pends - padded
    dest = (pstart[top_i[:, :TOP_K]] + ranks[:, :TOP_K]).reshape(-1).astype(jnp.int32)
    n_used = (pends[-1:] // MOE_ROWS).astype(jnp.int32)
    blk_e = jnp.minimum(
        jnp.searchsorted(pends, jnp.arange(n_rows // MOE_ROWS, dtype=jnp.int32) * MOE_ROWS,
                         side="right"), N_EXPERTS - 1).astype(jnp.int32)

    xs = _dispatch(h2, dest, (pstart + sizes).astype(jnp.int32), (padded - sizes).astype(jnp.int32),
                   n_used, n_rows)
    y = _experts(xs, blk_e, n_used, p["wg"], p["wl"], p["bg"], p["bl"], p["wd"], p["bd"])
    out = _combine(y, dest, x1, gates, p["norm_final"])
    return out.reshape(B, S, D_MODEL)


def kernel(x_prompt, x_sample, norm_mix, w_in, lambda_q1, lambda_k1, lambda_q2, lambda_k2, subln_w,
           lb_fwd, lb_bwd, hgrn_norm, w_out, norm_ffn, w_router, b_router, w_gate_up, b_gate_up,
           w_down, b_down, norm_final):
    l = 0
    lam = (jnp.exp(jnp.sum(lambda_q1[l] * lambda_k1[l])) - jnp.exp(jnp.sum(lambda_q2[l] * lambda_k2[l]))
           + LAM_INIT)
    wgu = w_gate_up[l]
    bgu = b_gate_up[l]
    p = {
        "norm_mix": norm_mix[l][None, :],
        "w_in": w_in[l].astype(BF16),
        "lam": lam.reshape(1).astype(F32),
        "subln_w": subln_w[l][None, :],
        "lb_fwd": jnp.cumsum(jax.nn.softmax(lb_fwd, axis=0), axis=0)[l][None, :],
        "lb_bwd": jnp.cumsum(jax.nn.softmax(lb_bwd, axis=0), axis=0)[l][None, :],
        "hgrn_norm": hgrn_norm[l][None, :],
        "w_out": w_out[l].astype(BF16),
        "norm_ffn": norm_ffn[l][None, :],
        "w_router": jnp.pad(w_router[l], ((0, 0), (0, LANES - N_EXPERTS))),
        "b_router": jnp.pad(b_router[l], (0, LANES - N_EXPERTS), constant_values=-jnp.inf)[None, :],
        "wg": wgu[:, :, 0::2].astype(BF16),
        "wl": wgu[:, :, 1::2].astype(BF16),
        "bg": bgu[:, None, 0::2],
        "bl": bgu[:, None, 1::2],
        "wd": w_down[l].astype(BF16),
        "bd": b_down[l][:, None, :],
        "norm_final": norm_final[None, :],
    }
    return (_trunk(x_prompt, p), _trunk(x_sample, p))
```

```python
import functools
import math

import jax
import jax.numpy as jnp
from jax import lax
from jax.experimental import pallas as pl
from jax.experimental.pallas import tpu as pltpu

D_MODEL = 1024
HEAD_DIM = 64
ATT_HEADS = 4
ATT_V = 2 * HEAD_DIM
ATT_WIDTH = ATT_HEADS * ATT_V
ROT_DIM = HEAD_DIM // 4
ROPE_THETA = 500000.0
HG_HEADS = 4
HG_K = 128
HG_WIDTH = HG_HEADS * HG_K
IN_COLS = 4096
GROUP_COLS = 512
N_EXPERTS = 32
TOP_K = 4
D_FF = 1024
SWIGLU_LIMIT = 7.0
SWIGLU_ALPHA = 1.702
EPS = 1e-5
LAM_INIT = 0.8 - 0.6 * math.exp(-0.3 * 0)

LANES = 128
SUBLANES = 8
RUN_ALIGN = SUBLANES
DMA_ROWS = (128, 32, 8)
HG_CHUNK = 128
ATT_TK = 256
ATT_UNROLL = 8
LOG2_E = math.log2(math.e)
MOE_ROWS = 512
VMEM_LIMIT = 56 * 1024 * 1024

F32 = jnp.float32
BF16 = jnp.bfloat16
NT_DIMS = (((1,), (1,)), ((), ()))
TN_DIMS = (((0,), (0,)), ((), ()))


def _params(*sem):
    return pltpu.CompilerParams(dimension_semantics=sem, vmem_limit_bytes=VMEM_LIMIT)


def _inproj_kernel(x_ref, nw_ref, w_ref, cos_ref, sa_ref, sb_ref, lbf_ref, lbb_ref,
                   q_ref, k_ref, v_ref, qh_ref, lff_ref, lfb_ref, ih_ref, g_ref):
    x = x_ref[...]
    ms = jnp.mean(x * x, axis=-1, keepdims=True)
    h = (x * lax.rsqrt(ms + EPS) * nw_ref[...]).astype(BF16)

    def proj(c):
        return jnp.dot(h, w_ref[:, c * GROUP_COLS:(c + 1) * GROUP_COLS],
                       preferred_element_type=F32)

    cos, sa, sb = cos_ref[...], sa_ref[...], sb_ref[...]

    def rope(p):
        outs = []
        for j in range(GROUP_COLS // LANES):
            xx = p[:, j * LANES:(j + 1) * LANES]
            outs.append(xx * cos + pltpu.roll(xx, LANES - ROT_DIM // 2, 1) * sa
                        + pltpu.roll(xx, ROT_DIM // 2, 1) * sb)
        return jnp.concatenate(outs, axis=1)

    def log_forget(p, lb):
        return jnp.log(lb + (1.0 - lb) * jax.nn.sigmoid(p))

    q_ref[...] = (rope(proj(0)) * (HEAD_DIM ** -0.5 * LOG2_E)).astype(BF16)
    k_ref[...] = rope(proj(1)).astype(BF16)
    va = proj(2).astype(BF16)
    ones = jnp.ones((va.shape[0], ATT_V), BF16)
    for hh in range(ATT_HEADS):
        v_ref[:, 2 * hh * ATT_V:(2 * hh + 1) * ATT_V] = va[:, hh * ATT_V:(hh + 1) * ATT_V]
        v_ref[:, (2 * hh + 1) * ATT_V:(2 * hh + 2) * ATT_V] = ones
    qh_ref[...] = proj(3).astype(BF16)
    lff_ref[...] = log_forget(proj(4), lbf_ref[...])
    lfb_ref[...] = log_forget(proj(5), lbb_ref[...])
    ih_ref[...] = proj(6).astype(BF16)
    gh = proj(7)
    g_ref[...] = (gh * jax.nn.sigmoid(gh)).astype(BF16)


def _inproj(x2d, seq, norm_w, w_in, cos_t, sa_t, sb_t, lbf, lbb):
    T = x2d.shape[0]
    tm = min(512, seq)
    nseq = seq // tm
    row = lambda i: (i, 0)
    fixed = lambda i: (0, 0)
    tab = lambda i: (i % nseq, 0)
    out_dt = [BF16, BF16, BF16, BF16, F32, F32, BF16, BF16]
    out_w = [GROUP_COLS, GROUP_COLS, 2 * ATT_WIDTH] + [GROUP_COLS] * 5
    return pl.pallas_call(
        _inproj_kernel,
        grid=(T // tm,),
        in_specs=[
            pl.BlockSpec((tm, D_MODEL), row),
            pl.BlockSpec((1, D_MODEL), fixed),
            pl.BlockSpec((D_MODEL, IN_COLS), fixed),
            pl.BlockSpec((tm, LANES), tab),
            pl.BlockSpec((tm, LANES), tab),
            pl.BlockSpec((tm, LANES), tab),
            pl.BlockSpec((1, GROUP_COLS), fixed),
            pl.BlockSpec((1, GROUP_COLS), fixed),
        ],
        out_specs=[pl.BlockSpec((tm, w), row) for w in out_w],
        out_shape=[jax.ShapeDtypeStruct((T, w), dt) for w, dt in zip(out_w, out_dt)],
        compiler_params=_params("parallel"),
        name="inproj",
    )(x2d, norm_w, w_in, cos_t, sa_t, sb_t, lbf, lbb)


def _attn_kernel(lam_ref, q_ref, k_ref, v_ref, w_ref, o_ref, m_s, acc_s):
    n_chunks = k_ref.shape[0] // ATT_TK
    q = q_ref[...]
    lane = lax.broadcasted_iota(jnp.int32, q.shape, 1)
    zero = jnp.zeros_like(q)
    qz = (jnp.where(lane < HEAD_DIM, q, zero), jnp.where(lane >= HEAD_DIM, q, zero))
    m_s[...] = jnp.full(m_s.shape, -jnp.inf, F32)
    acc_s[...] = jnp.zeros(acc_s.shape, F32)

    def chunk(j):
        start = pl.multiple_of(j * ATT_TK, ATT_TK)
        kc = k_ref[pl.ds(start, ATT_TK), :]
        vc = v_ref[pl.ds(start, ATT_TK), :]
        for c in range(2):
            s = lax.dot_general(qz[c], kc, NT_DIMS, preferred_element_type=F32)
            m_prev = m_s[c]
            m_next = jnp.maximum(m_prev, jnp.max(s, axis=1, keepdims=True))
            alpha = jnp.exp2(m_prev - m_next)
            p = jnp.exp2(s - jnp.tile(m_next, (1, ATT_TK // LANES))).astype(BF16)
            acc_s[c] = acc_s[c] * jnp.tile(alpha, (1, 2)) + jnp.dot(p, vc, preferred_element_type=F32)
            m_s[c] = m_next

    unroll = math.gcd(n_chunks, ATT_UNROLL)

    def body(jj, carry):
        for u in range(unroll):
            chunk(jj * unroll + u)
        return carry

    lax.fori_loop(0, n_chunks // unroll, body, 0)

    lam = lam_ref[0]
    a1, a2 = acc_s[0], acc_s[1]
    o = a1[:, :ATT_V] / a1[:, ATT_V:] - lam * (a2[:, :ATT_V] / a2[:, ATT_V:])
    ms = jnp.mean(o * o, axis=-1, keepdims=True)
    o = o * lax.rsqrt(ms + EPS) * w_ref[...] * (1.0 - LAM_INIT)
    o_ref[...] = o.astype(o_ref.dtype)


def _attention(q, k, v_ext, lam, subln_w):
    B, S, _ = q.shape
    tq = min(512, S)
    return pl.pallas_call(
        _attn_kernel,
        grid=(B, ATT_HEADS, S // tq),
        in_specs=[
            pl.BlockSpec(memory_space=pltpu.SMEM),
            pl.BlockSpec((None, tq, ATT_V), lambda b, h, i: (b, i, h)),
            pl.BlockSpec((None, S, ATT_V), lambda b, h, i: (b, 0, h)),
            pl.BlockSpec((None, S, 2 * ATT_V), lambda b, h, i: (b, 0, h)),
            pl.BlockSpec((1, ATT_V), lambda b, h, i: (0, 0)),
        ],
        out_specs=pl.BlockSpec((None, tq, ATT_V), lambda b, h, i: (b, i, h)),
        out_shape=jax.ShapeDtypeStruct((B, S, ATT_WIDTH), BF16),
        scratch_shapes=[
            pltpu.VMEM((2, tq, LANES), F32),
            pltpu.VMEM((2, tq, 2 * ATT_V), F32),
        ],
        compiler_params=_params("parallel", "parallel", "arbitrary"),
        name="diff_attention",
    )(lam, q, k, v_ext, subln_w)


def _hgrn_kernel(q_ref, lf_ref, v_ref, lvl_ref, o_ref, st_ref, *, reverse):
    C = q_ref.shape[0]

    @pl.when(pl.program_id(1) == 0)
    def _():
        st_ref[...] = jnp.zeros(st_ref.shape, F32)

    lf = lf_ref[...]
    q = q_ref[...].astype(F32)
    v = v_ref[...]
    kk = 1.0 - jnp.exp(lf)
    row = lax.broadcasted_iota(jnp.int32, lf.shape, 0)
    hi_side, lo_side = (kk, q) if reverse else (q, kk)

    pref, tot = lf, lf
    xs = []
    m = 1
    while m < C:
        upper = (row & m) != 0
        if reverse:
            g = jnp.where(upper, pref - lf, tot - pref + lf)
        else:
            g = jnp.where(upper, pref, tot - pref)
        xs.append((jnp.where(upper, hi_side, lo_side) * jnp.exp(g)).astype(BF16))
        below = pltpu.roll(tot, m, 0)
        above = pltpu.roll(tot, C - m, 0)
        pref = jnp.where(upper, pref + below, pref)
        tot = tot + jnp.where(upper, below, above)
        m *= 2

    if reverse:
        gq, gk = tot - pref + lf, pref - lf
    else:
        gq, gk = pref, tot - pref
    xq = (q * jnp.exp(gq)).astype(BF16)
    xk = (kk * jnp.exp(gk)).astype(BF16)
    chunk_decay = jnp.exp(tot[0:1, :])
    qk = q * kk
    lvl = lvl_ref[...]

    for h in range(HG_HEADS):
        sl = slice(h * HG_K, (h + 1) * HG_K)
        a = jnp.zeros((C, C), F32)
        for li, x in enumerate(xs):
            xh = x[:, sl]
            am = lax.dot_general(xh, xh, NT_DIMS, preferred_element_type=F32)
            a = jnp.where(lvl == li + 1, am, a)
        vh = v[:, sl]
        st = st_ref[h]
        o = jnp.dot(a.astype(BF16), vh, preferred_element_type=F32)
        o += lax.dot_general(xq[:, sl], st.astype(BF16), NT_DIMS, preferred_element_type=F32)
        o += jnp.sum(qk[:, sl], axis=1, keepdims=True) * vh.astype(F32)
        o_ref[:, sl] = o
        st_ref[h] = st * chunk_decay[:, sl] + lax.dot_general(
            vh, xk[:, sl], TN_DIMS, preferred_element_type=F32)


def _level_table(reverse):
    t = jnp.arange(HG_CHUNK, dtype=jnp.int32)[:, None]
    s = jnp.arange(HG_CHUNK, dtype=jnp.int32)[None, :]
    x = t ^ s
    lvl = jnp.zeros_like(x)
    for b in range(HG_CHUNK.bit_length() - 1):
        lvl = jnp.where(x >= (1 << b), b + 1, lvl)
    active = (s > t) if reverse else (s < t)
    return jnp.where(active, lvl, 0)


def _hgrn(qh, lf, vh, reverse):
    B, S, _ = qh.shape
    C = HG_CHUNK
    nc = S // C
    if reverse:
        blk = lambda b, c: (b, nc - 1 - c, 0)
    else:
        blk = lambda b, c: (b, c, 0)
    spec = pl.BlockSpec((None, C, HG_WIDTH), blk)
    return pl.pallas_call(
        functools.partial(_hgrn_kernel, reverse=reverse),
        grid=(B, nc),
        in_specs=[spec, spec, spec, pl.BlockSpec((C, C), lambda b, c: (0, 0))],
        out_specs=spec,
        out_shape=jax.ShapeDtypeStruct((B, S, HG_WIDTH), F32),
        scratch_shapes=[pltpu.VMEM((HG_HEADS, HG_K, HG_K), F32)],
        compiler_params=_params("parallel", "arbitrary"),
        name="hgrn_bwd" if reverse else "hgrn_fwd",
    )(qh, lf, vh, _level_table(reverse))


def _outproj_kernel(x_ref, oa_ref, of_ref, ob_ref, g_ref, hn_ref, wo_ref, fn_ref, wr_ref, br_ref,
                    x1_ref, h2_ref, ti_ref, gate_ref, rank_ref, cnt_ref):
    tm = x_ref.shape[0]

    oh = of_ref[...] + ob_ref[...]
    ms = jnp.mean(oh * oh, axis=-1, keepdims=True)
    ohn = oh * lax.rsqrt(ms + EPS) * hn_ref[...] * g_ref[...].astype(F32)
    mix = jnp.dot(oa_ref[...], wo_ref[:ATT_WIDTH, :], preferred_element_type=F32)
    mix += jnp.dot(ohn.astype(BF16), wo_ref[ATT_WIDTH:, :], preferred_element_type=F32)
    x1 = x_ref[...] + mix
    x1_ref[...] = x1
    ms = jnp.mean(x1 * x1, axis=-1, keepdims=True)
    h2 = x1 * lax.rsqrt(ms + EPS) * fn_ref[...]
    h2_ref[...] = h2.astype(BF16)

    logits = jnp.dot(h2, wr_ref[...], preferred_element_type=F32,
                     precision=lax.Precision.HIGHEST) + br_ref[...]
    lane = lax.broadcasted_iota(jnp.int32, logits.shape, 1)
    vals, idxs = [], []
    for _ in range(TOP_K):
        mx = jnp.max(logits, axis=1, keepdims=True)
        ix = jnp.min(jnp.where(logits == mx, lane, LANES), axis=1, keepdims=True)
        vals.append(mx)
        idxs.append(ix)
        logits = jnp.where(lane == ix, -jnp.inf, logits)
    es = [jnp.exp(vv - vals[0]) for vv in vals]
    den = es[0] + es[1] + es[2] + es[3]

    onehots = [(lane == ix).astype(F32) for ix in idxs]
    multi = onehots[0] + onehots[1] + onehots[2] + onehots[3]
    r_i = lax.broadcasted_iota(jnp.int32, (tm, tm), 0)
    c_i = lax.broadcasted_iota(jnp.int32, (tm, tm), 1)
    tri = jnp.where(c_i < r_i, 1.0, 0.0).astype(BF16)
    before = jnp.dot(tri, multi.astype(BF16), preferred_element_type=F32)

    ti = jnp.full(logits.shape, -1, jnp.int32)
    gates = jnp.zeros(logits.shape, F32)
    ranks = jnp.zeros(logits.shape, jnp.int32)
    for kslot in range(TOP_K):
        here = lane == kslot
        ti = jnp.where(here, idxs[kslot], ti)
        gates = jnp.where(here, es[kslot] / den, gates)
        rk = jnp.sum(onehots[kslot] * before, axis=1, keepdims=True)
        ranks = jnp.where(here, rk.astype(jnp.int32), ranks)
    ti_ref[...] = ti
    gate_ref[...] = gates
    rank_ref[...] = ranks
    cnt_ref[...] = jnp.broadcast_to(jnp.sum(multi, axis=0, keepdims=True), cnt_ref.shape)


def _outproj(x2d, oa, o_f, o_b, g, hgrn_norm, w_out, norm_ffn, w_router, b_router):
    T = x2d.shape[0]
    tm = _moe_tile(T)
    row = lambda i: (i, 0)
    fixed = lambda i: (0, 0)
    return pl.pallas_call(
        _outproj_kernel,
        grid=(T // tm,),
        in_specs=[
            pl.BlockSpec((tm, D_MODEL), row),
            pl.BlockSpec((tm, ATT_WIDTH), row),
            pl.BlockSpec((tm, HG_WIDTH), row),
            pl.BlockSpec((tm, HG_WIDTH), row),
            pl.BlockSpec((tm, HG_WIDTH), row),
            pl.BlockSpec((1, HG_WIDTH), fixed),
            pl.BlockSpec((D_MODEL, D_MODEL), fixed),
            pl.BlockSpec((1, D_MODEL), fixed),
            pl.BlockSpec((D_MODEL, LANES), fixed),
            pl.BlockSpec((1, LANES), fixed),
        ],
        out_specs=[
            pl.BlockSpec((tm, D_MODEL), row),
            pl.BlockSpec((tm, D_MODEL), row),
            pl.BlockSpec((tm, LANES), row),
            pl.BlockSpec((tm, LANES), row),
            pl.BlockSpec((tm, LANES), row),
            pl.BlockSpec((None, SUBLANES, LANES), lambda i: (i, 0, 0)),
        ],
        out_shape=[
            jax.ShapeDtypeStruct((T, D_MODEL), F32),
            jax.ShapeDtypeStruct((T, D_MODEL), BF16),
            jax.ShapeDtypeStruct((T, LANES), jnp.int32),
            jax.ShapeDtypeStruct((T, LANES), F32),
            jax.ShapeDtypeStruct((T, LANES), jnp.int32),
            jax.ShapeDtypeStruct((T // tm, SUBLANES, LANES), F32),
        ],
        compiler_params=_params("parallel"),
        name="outproj_router",
    )(x2d, oa, o_f, o_b, g, hgrn_norm, w_out, norm_ffn, w_router, b_router)


def _moe_tile(T):
    return min(512, T)


def _stage_rows(tm):
    return TOP_K * tm + N_EXPERTS * RUN_ALIGN


def _run_dmas(tile, run_start_ref, run_len_ref, stage_off_ref, make_copy, act):
    def per_expert(e, carry):
        n = run_len_ref[tile * N_EXPERTS + e]
        src = stage_off_ref[tile * N_EXPERTS + e]
        dst = run_start_ref[tile * N_EXPERTS + e]
        done = 0
        for size in DMA_ROWS:
            cnt = (n - done) // size

            def one(j, c, size=size, done=done):
                off = pl.multiple_of(done + j * size, RUN_ALIGN)
                act(make_copy(pl.multiple_of(src + off, RUN_ALIGN),
                              pl.multiple_of(dst + off, RUN_ALIGN), size))
                return c

            lax.fori_loop(0, cnt, one, 0)
            done = done + cnt * size
        return carry

    lax.fori_loop(0, N_EXPERTS, per_expert, 0)


def _stage_rows_of(ti_ref, rank_ref, soff_ref):
    ti = ti_ref[...]
    ranks = rank_ref[...]
    soff = soff_ref[0:1, :]
    lane = lax.broadcasted_iota(jnp.int32, ti.shape, 1)
    cols = []
    for kslot in range(TOP_K):
        e_k = jnp.sum(jnp.where(lane == kslot, ti, 0), axis=1, keepdims=True)
        r_k = jnp.sum(jnp.where(lane == kslot, ranks, 0), axis=1, keepdims=True)
        off_k = jnp.sum(jnp.where(lane == e_k, soff, 0), axis=1, keepdims=True)
        cols.append(off_k + r_k)
    return cols


def _dispatch_kernel(run_start_ref, run_len_ref, stage_off_ref, pad_start_ref, pad_len_ref, nu_ref,
                     h2_ref, ti_ref, rank_ref, soff_ref, xs_ref, stage, zero_s, sems, zsem):
    i = pl.program_id(0)
    nt = pl.num_programs(0)
    slot = i % 2
    tm = h2_ref.shape[0]
    sr = stage.shape[1]

    def copy_out(s):
        def make(stage_row, hbm_row, size):
            return pltpu.make_async_copy(stage.at[s, pl.ds(stage_row, size)],
                                         xs_ref.at[pl.ds(hbm_row, size)], sems.at[s])
        return make

    tables = (run_start_ref, run_len_ref, stage_off_ref)

    for s in range(2):
        @pl.when((i >= 2) & (slot == s))
        def _(s=s):
            _run_dmas(i - 2, *tables, copy_out(s), lambda c: c.wait())

    @pl.when(i == 0)
    def _():
        zero_s[...] = jnp.zeros(zero_s.shape, F32)

        def zero_copy(hbm_row, size):
            return pltpu.make_async_copy(zero_s.at[pl.ds(0, size)], xs_ref.at[pl.ds(hbm_row, size)], zsem)

        for e in range(N_EXPERTS):
            done = 0
            for size in DMA_ROWS:
                cnt = (pad_len_ref[e] - done) // size

                def fill(j, c, e=e, size=size, done=done):
                    zero_copy(pl.multiple_of(pad_start_ref[e] + done + j * size, RUN_ALIGN), size).start()
                    return c

                def drain(j, c, e=e, size=size, done=done):
                    zero_copy(pl.multiple_of(pad_start_ref[e] + done + j * size, RUN_ALIGN), size).wait()
                    return c

                lax.fori_loop(0, cnt, fill, 0)
                lax.fori_loop(0, cnt, drain, 0)
                done = done + cnt * size

        per_blk = MOE_ROWS // DMA_ROWS[0]

        def fill_tail(j, c):
            zero_copy(pl.multiple_of(j * DMA_ROWS[0], RUN_ALIGN), DMA_ROWS[0]).start()
            return c

        def drain_tail(j, c):
            zero_copy(pl.multiple_of(j * DMA_ROWS[0], RUN_ALIGN), DMA_ROWS[0]).wait()
            return c

        lo, hi = nu_ref[0] * per_blk, (xs_ref.shape[0] // MOE_ROWS) * per_blk
        lax.fori_loop(lo, hi, fill_tail, 0)
        lax.fori_loop(lo, hi, drain_tail, 0)

    cols = _stage_rows_of(ti_ref, rank_ref, soff_ref)
    r_iota = lax.broadcasted_iota(jnp.int32, (tm, sr), 1)
    hit = r_iota == cols[0]
    for c in cols[1:]:
        hit = hit | (r_iota == c)
    onehot = jnp.where(hit, 1.0, 0.0).astype(BF16)
    staged = lax.dot_general(onehot, h2_ref[...], TN_DIMS, preferred_element_type=F32)
    for s in range(2):
        @pl.when(slot == s)
        def _(s=s):
            stage[s] = staged
            _run_dmas(i, *tables, copy_out(s), lambda c: c.start())

    @pl.when(i == nt - 1)
    def _():
        for s in range(2):
            @pl.when((nt >= 2) & (slot != s))
            def _(s=s):
                _run_dmas(i - 1, *tables, copy_out(s), lambda c: c.wait())

            @pl.when(slot == s)
            def _(s=s):
                _run_dmas(i, *tables, copy_out(s), lambda c: c.wait())


def _dispatch(h2, top_i, ranks, soff_tab, run_start, run_len, stage_off, pad_start, pad_len, n_used,
              n_rows):
    T = h2.shape[0]
    tm = _moe_tile(T)
    row = lambda i, *_: (i, 0)
    return pl.pallas_call(
        _dispatch_kernel,
        grid_spec=pltpu.PrefetchScalarGridSpec(
            num_scalar_prefetch=6,
            grid=(T // tm,),
            in_specs=[
                pl.BlockSpec((tm, D_MODEL), row),
                pl.BlockSpec((tm, LANES), row),
                pl.BlockSpec((tm, LANES), row),
                pl.BlockSpec((None, SUBLANES, LANES), lambda i, *_: (i, 0, 0)),
            ],
            out_specs=pl.BlockSpec(memory_space=pl.ANY),
            scratch_shapes=[
                pltpu.VMEM((2, _stage_rows(tm), D_MODEL), F32),
                pltpu.VMEM((DMA_ROWS[0], D_MODEL), F32),
                pltpu.SemaphoreType.DMA((2,)),
                pltpu.SemaphoreType.DMA(()),
            ],
        ),
        out_shape=jax.ShapeDtypeStruct((n_rows, D_MODEL), F32),
        compiler_params=_params("arbitrary"),
        name="moe_dispatch",
    )(run_start, run_len, stage_off, pad_start, pad_len, n_used, h2, top_i, ranks, soff_tab)


def _expert_kernel(be_ref, nu_ref, xs_ref, wg_ref, wl_ref, bg_ref, bl_ref, wd_ref, bd_ref, y_ref):
    @pl.when(pl.program_id(0) < nu_ref[0])
    def _():
        x = xs_ref[...].astype(BF16)
        glu = jnp.dot(x, wg_ref[...], preferred_element_type=F32) + bg_ref[...]
        lin = jnp.dot(x, wl_ref[...], preferred_element_type=F32) + bl_ref[...]
        glu = jnp.minimum(glu, SWIGLU_LIMIT)
        lin = jnp.clip(lin, -SWIGLU_LIMIT, SWIGLU_LIMIT)
        act = glu * jax.nn.sigmoid(SWIGLU_ALPHA * glu) * (lin + 1.0)
        y_ref[...] = jnp.dot(act.astype(BF16), wd_ref[...], preferred_element_type=F32) + bd_ref[...]

    @pl.when(pl.program_id(0) >= nu_ref[0])
    def _():
        y_ref[...] = jnp.zeros(y_ref.shape, F32)


def _experts(xs, blk_e, n_used, wg, wl, bg, bl, wd, bd):
    n_rows = xs.shape[0]
    nblk = n_rows // MOE_ROWS
    rows = lambda i, be, nu: (jnp.minimum(i, nu[0] - 1), 0)
    per_e = lambda i, be, nu: (be[jnp.minimum(i, nu[0] - 1)], 0, 0)
    return pl.pallas_call(
        _expert_kernel,
        grid_spec=pltpu.PrefetchScalarGridSpec(
            num_scalar_prefetch=2,
            grid=(nblk,),
            in_specs=[
                pl.BlockSpec((MOE_ROWS, D_MODEL), rows),
                pl.BlockSpec((None, D_MODEL, D_FF), per_e),
                pl.BlockSpec((None, D_MODEL, D_FF), per_e),
                pl.BlockSpec((None, 1, D_FF), per_e),
                pl.BlockSpec((None, 1, D_FF), per_e),
                pl.BlockSpec((None, D_FF, D_MODEL), per_e),
                pl.BlockSpec((None, 1, D_MODEL), per_e),
            ],
            out_specs=pl.BlockSpec((MOE_ROWS, D_MODEL), lambda i, be, nu: (i, 0)),
        ),
        out_shape=jax.ShapeDtypeStruct((n_rows, D_MODEL), F32),
        compiler_params=_params("arbitrary"),
        name="moe_experts",
    )(blk_e, n_used, xs, wg, wl, bg, bl, wd, bd)


def _split_kernel(w_ref, perm_ref, g_ref, l_ref):
    w = w_ref[...].astype(BF16)
    r = jnp.dot(w, perm_ref[...], preferred_element_type=F32)
    g_ref[...] = r[:, :LANES].astype(BF16)
    l_ref[...] = r[:, LANES:].astype(BF16)


def _split_gate_up(w_gate_up):
    E, D, F2 = w_gate_up.shape
    grp = 2 * LANES
    src = jnp.arange(grp, dtype=jnp.int32)[:, None]
    dst = jnp.arange(grp, dtype=jnp.int32)[None, :]
    perm = (dst == (src // 2 + LANES * (src % 2))).astype(BF16)
    half = jax.ShapeDtypeStruct((E, D, F2 // 2), BF16)
    out_spec = pl.BlockSpec((None, D, LANES), lambda e, j: (e, 0, j))
    return pl.pallas_call(
        _split_kernel,
        grid=(E, F2 // grp),
        in_specs=[pl.BlockSpec((None, D, grp), lambda e, j: (e, 0, j)),
                  pl.BlockSpec((grp, grp), lambda e, j: (0, 0))],
        out_specs=[out_spec, out_spec],
        out_shape=[half, half],
        compiler_params=_params("parallel", "parallel"),
        name="split_gate_up",
    )(w_gate_up, perm)


def _combine_kernel(run_start_ref, run_len_ref, stage_off_ref,
                    y_ref, x1_ref, ti_ref, rank_ref, gate_ref, soff_ref, nf_ref, o_ref, stage, sems):
    i = pl.program_id(0)
    nt = pl.num_programs(0)
    slot = i % 2
    tm = x1_ref.shape[0]
    sr = stage.shape[1]

    def copy_in(s):
        def make(stage_row, hbm_row, size):
            return pltpu.make_async_copy(y_ref.at[pl.ds(hbm_row, size)],
                                         stage.at[s, pl.ds(stage_row, size)], sems.at[s])
        return make

    tables = (run_start_ref, run_len_ref, stage_off_ref)

    @pl.when(i == 0)
    def _():
        stage[...] = jnp.zeros(stage.shape, F32)
        _run_dmas(i, *tables, copy_in(0), lambda c: c.start())

    for s in range(2):
        @pl.when((i + 1 < nt) & (slot != s))
        def _(s=s):
            _run_dmas(i + 1, *tables, copy_in(s), lambda c: c.start())

    for s in range(2):
        @pl.when(slot == s)
        def _(s=s):
            _run_dmas(i, *tables, copy_in(s), lambda c: c.wait())

    cols = _stage_rows_of(ti_ref, rank_ref, soff_ref)
    gates = gate_ref[...]
    lane = lax.broadcasted_iota(jnp.int32, gates.shape, 1)
    r_iota = lax.broadcasted_iota(jnp.int32, (tm, sr), 1)
    weights = jnp.zeros((tm, sr), F32)
    for kslot, c in enumerate(cols):
        g_k = jnp.sum(jnp.where(lane == kslot, gates, 0.0), axis=1, keepdims=True)
        weights = jnp.where(r_iota == c, g_k, weights)
    staged = stage[slot].astype(BF16)
    acc = x1_ref[...] + jnp.dot(weights.astype(BF16), staged, preferred_element_type=F32)
    ms = jnp.mean(acc * acc, axis=-1, keepdims=True)
    o_ref[...] = acc * lax.rsqrt(ms + EPS) * nf_ref[...]


def _combine(y, x1, top_i, ranks, gates, soff_tab, run_start, run_len, stage_off, norm_final):
    T = x1.shape[0]
    tm = _moe_tile(T)
    row = lambda i, *_: (i, 0)
    return pl.pallas_call(
        _combine_kernel,
        grid_spec=pltpu.PrefetchScalarGridSpec(
            num_scalar_prefetch=3,
            grid=(T // tm,),
            in_specs=[
                pl.BlockSpec(memory_space=pl.ANY),
                pl.BlockSpec((tm, D_MODEL), row),
                pl.BlockSpec((tm, LANES), row),
                pl.BlockSpec((tm, LANES), row),
                pl.BlockSpec((tm, LANES), row),
                pl.BlockSpec((None, SUBLANES, LANES), lambda i, *_: (i, 0, 0)),
                pl.BlockSpec((1, D_MODEL), lambda i, *_: (0, 0)),
            ],
            out_specs=pl.BlockSpec((tm, D_MODEL), row),
            scratch_shapes=[
                pltpu.VMEM((2, _stage_rows(tm), D_MODEL), F32),
                pltpu.SemaphoreType.DMA((2,)),
            ],
        ),
        out_shape=jax.ShapeDtypeStruct((T, D_MODEL), F32),
        compiler_params=_params("arbitrary"),
        name="moe_combine",
    )(run_start, run_len, stage_off, y, x1, top_i, ranks, gates, soff_tab, norm_final)


def _rope_tables(seq):
    half = ROT_DIM // 2
    inv = ROPE_THETA ** (-jnp.arange(0, ROT_DIM, 2, dtype=F32) / ROT_DIM)
    ang = jnp.arange(seq, dtype=F32)[:, None] * inv[None, :]
    cos, sin = jnp.cos(ang), jnp.sin(ang)
    pad = jnp.zeros((seq, HEAD_DIM - ROT_DIM), F32)
    zero = jnp.zeros((seq, half), F32)
    cos_t = jnp.concatenate([cos, cos, pad + 1.0], axis=1)
    sa_t = jnp.concatenate([-sin, zero, pad], axis=1)
    sb_t = jnp.concatenate([zero, sin, pad], axis=1)
    rep = LANES // HEAD_DIM
    return tuple(jnp.tile(t, (1, rep)) for t in (cos_t, sa_t, sb_t))


def _trunk(x, p):
    B, S, _ = x.shape
    T = B * S
    x2d = x.reshape(T, D_MODEL)
    q, k, v, qh, lff, lfb, ih, g = _inproj(x2d, S, p["norm_mix"], p["w_in"], *_rope_tables(S),
                                           p["lb_fwd"], p["lb_bwd"])
    to3 = lambda a: a.reshape(B, S, a.shape[-1])
    oa = _attention(to3(q), to3(k), to3(v), p["lam"], p["subln_w"])
    o_f = _hgrn(to3(qh), to3(lff), to3(ih), reverse=False)
    o_b = _hgrn(to3(qh), to3(lfb), to3(ih), reverse=True)
    x1, h2, top_i, gates, ranks, counts = _outproj(
        x2d, oa.reshape(T, ATT_WIDTH), o_f.reshape(T, HG_WIDTH), o_b.reshape(T, HG_WIDTH), g,
        p["hgrn_norm"], p["w_out"], p["norm_ffn"], p["w_router"], p["b_router"])

    tm = _moe_tile(T)
    nt = T // tm
    n_rows = (pl.cdiv(T * TOP_K + nt * N_EXPERTS * RUN_ALIGN, MOE_ROWS) + N_EXPERTS) * MOE_ROWS
    cnt = counts[:, 0, :N_EXPERTS].astype(jnp.int32)
    run_len = (cnt + RUN_ALIGN - 1) // RUN_ALIGN * RUN_ALIGN
    sizes = jnp.sum(run_len, axis=0)
    padded = (sizes + MOE_ROWS - 1) // MOE_ROWS * MOE_ROWS
    pends = jnp.cumsum(padded)
    pstart = pends - padded
    run_start = pstart[None, :] + jnp.cumsum(run_len, axis=0) - run_len
    stage_off = jnp.cumsum(run_len, axis=1) - run_len
    soff_tab = jnp.broadcast_to(
        jnp.pad(stage_off, ((0, 0), (0, LANES - N_EXPERTS)))[:, None, :], (nt, SUBLANES, LANES))
    n_used = (pends[-1:] // MOE_ROWS).astype(jnp.int32)
    blk_e = jnp.minimum(
        jnp.searchsorted(pends, jnp.arange(n_rows // MOE_ROWS, dtype=jnp.int32) * MOE_ROWS,
                         side="right"), N_EXPERTS - 1).astype(jnp.int32)
    flat = lambda a: a.reshape(-1).astype(jnp.int32)

    xs = _dispatch(h2, top_i, ranks, soff_tab, flat(run_start), flat(run_len), flat(stage_off),
                   flat(pstart + sizes), flat(padded - sizes), n_used, n_rows)
    y = _experts(xs, blk_e, n_used, p["wg"], p["wl"], p["bg"], p["bl"], p["wd"], p["bd"])
    out = _combine(y, x1, top_i, ranks, gates, soff_tab, flat(run_start), flat(run_len),
                   flat(stage_off), p["norm_final"])
    return out.reshape(B, S, D_MODEL)


def kernel(x_prompt, x_sample, norm_mix, w_in, lambda_q1, lambda_k1, lambda_q2, lambda_k2, subln_w,
           lb_fwd, lb_bwd, hgrn_norm, w_out, norm_ffn, w_router, b_router, w_gate_up, b_gate_up,
           w_down, b_down, norm_final):
    l = 0
    lam = (jnp.exp(jnp.sum(lambda_q1[l] * lambda_k1[l])) - jnp.exp(jnp.sum(lambda_q2[l] * lambda_k2[l]))
           + LAM_INIT)
    wg, wl = _split_gate_up(w_gate_up[l])
    bgu = b_gate_up[l]
    p = {
        "norm_mix": norm_mix[l][None, :],
        "w_in": w_in[l].astype(BF16),
        "lam": lam.reshape(1).astype(F32),
        "subln_w": subln_w[l][None, :],
        "lb_fwd": jnp.cumsum(jax.nn.softmax(lb_fwd, axis=0), axis=0)[l][None, :],
        "lb_bwd": jnp.cumsum(jax.nn.softmax(lb_bwd, axis=0), axis=0)[l][None, :],
        "hgrn_norm": hgrn_norm[l][None, :],
        "w_out": w_out[l].astype(BF16),
        "norm_ffn": norm_ffn[l][None, :],
        "w_router": jnp.pad(w_router[l], ((0, 0), (0, LANES - N_EXPERTS))),
        "b_router": jnp.pad(b_router[l], (0, LANES - N_EXPERTS), constant_values=-jnp.inf)[None, :],
        "wg": wg,
        "wl": wl,
        "bg": bgu[:, None, 0::2],
        "bl": bgu[:, None, 1::2],
        "wd": w_down[l].astype(BF16),
        "bd": b_down[l][:, None, :],
        "norm_final": norm_final[None, :],
    }
    return (_trunk(x_prompt, p), _trunk(x_sample, p))
```

```python
import math

import jax
import jax.numpy as jnp
from jax import lax
from jax.experimental import pallas as pl
from jax.experimental.pallas import tpu as pltpu

D_MODEL = 1024
HEAD_DIM = 64
ATT_HEADS = 4
ATT_V = 2 * HEAD_DIM
ATT_WIDTH = ATT_HEADS * ATT_V
ROT_DIM = HEAD_DIM // 4
ROPE_THETA = 500000.0
HG_HEADS = 4
HG_K = 128
HG_WIDTH = HG_HEADS * HG_K
IN_COLS = 4096
GROUP_COLS = 512
N_EXPERTS = 32
TOP_K = 4
D_FF = 1024
SWIGLU_LIMIT = 7.0
SWIGLU_ALPHA = 1.702
EPS = 1e-5
LAM_INIT = 0.8 - 0.6 * math.exp(-0.3 * 0)

LANES = 128
SUBLANES = 8
RUN_ALIGN = SUBLANES
DMA_ROWS = (128, 32, 8)
STAGE_BLK = 256
HG_CHUNK = 128
ATT_TK = 256
ATT_UNROLL = 16
LOG2_E = math.log2(math.e)
MOE_ROWS = 512
VMEM_LIMIT = 56 * 1024 * 1024

F32 = jnp.float32
BF16 = jnp.bfloat16
U32 = jnp.uint32
PACK_W = D_MODEL // 2
NT_DIMS = (((1,), (1,)), ((), ()))
TN_DIMS = (((0,), (0,)), ((), ()))


def _params(*sem):
    return pltpu.CompilerParams(dimension_semantics=sem, vmem_limit_bytes=VMEM_LIMIT)


def _inproj_kernel(x_ref, nw_ref, w_ref, cos_ref, sa_ref, sb_ref, lbf_ref, lbb_ref,
                   q_ref, k_ref, v_ref, qh_ref, lff_ref, lfb_ref, ih_ref, g_ref):
    x = x_ref[...]
    ms = jnp.mean(x * x, axis=-1, keepdims=True)
    h = (x * lax.rsqrt(ms + EPS) * nw_ref[...]).astype(BF16)

    def proj(c):
        return jnp.dot(h, w_ref[:, c * GROUP_COLS:(c + 1) * GROUP_COLS],
                       preferred_element_type=F32)

    cos, sa, sb = cos_ref[...], sa_ref[...], sb_ref[...]

    def rope(p):
        outs = []
        for j in range(GROUP_COLS // LANES):
            xx = p[:, j * LANES:(j + 1) * LANES]
            outs.append(xx * cos + pltpu.roll(xx, LANES - ROT_DIM // 2, 1) * sa
                        + pltpu.roll(xx, ROT_DIM // 2, 1) * sb)
        return jnp.concatenate(outs, axis=1)

    def log_forget(p, lb):
        return jnp.log(lb + (1.0 - lb) * jax.nn.sigmoid(p))

    q_ref[...] = (rope(proj(0)) * (HEAD_DIM ** -0.5 * LOG2_E)).astype(BF16)
    k_ref[...] = rope(proj(1)).astype(BF16)
    va = proj(2).astype(BF16)
    ones = jnp.ones((va.shape[0], ATT_V), BF16)
    for hh in range(ATT_HEADS):
        v_ref[:, 2 * hh * ATT_V:(2 * hh + 1) * ATT_V] = va[:, hh * ATT_V:(hh + 1) * ATT_V]
        v_ref[:, (2 * hh + 1) * ATT_V:(2 * hh + 2) * ATT_V] = ones
    qh_ref[...] = proj(3).astype(BF16)
    lff_ref[...] = log_forget(proj(4), lbf_ref[...])
    lfb_ref[...] = log_forget(proj(5), lbb_ref[...])
    ih_ref[...] = proj(6).astype(BF16)
    gh = proj(7)
    g_ref[...] = (gh * jax.nn.sigmoid(gh)).astype(BF16)


def _inproj(x2d, seq, norm_w, w_in, cos_t, sa_t, sb_t, lbf, lbb):
    T = x2d.shape[0]
    tm = min(512, seq)
    nseq = seq // tm
    row = lambda i: (i, 0)
    fixed = lambda i: (0, 0)
    tab = lambda i: (i % nseq, 0)
    out_dt = [BF16, BF16, BF16, BF16, F32, F32, BF16, BF16]
    out_w = [GROUP_COLS, GROUP_COLS, 2 * ATT_WIDTH] + [GROUP_COLS] * 5
    return pl.pallas_call(
        _inproj_kernel,
        grid=(T // tm,),
        in_specs=[
            pl.BlockSpec((tm, D_MODEL), row),
            pl.BlockSpec((1, D_MODEL), fixed),
            pl.BlockSpec((D_MODEL, IN_COLS), fixed),
            pl.BlockSpec((tm, LANES), tab),
            pl.BlockSpec((tm, LANES), tab),
            pl.BlockSpec((tm, LANES), tab),
            pl.BlockSpec((1, GROUP_COLS), fixed),
            pl.BlockSpec((1, GROUP_COLS), fixed),
        ],
        out_specs=[pl.BlockSpec((tm, w), row) for w in out_w],
        out_shape=[jax.ShapeDtypeStruct((T, w), dt) for w, dt in zip(out_w, out_dt)],
        compiler_params=_params("parallel"),
        name="inproj",
    )(x2d, norm_w, w_in, cos_t, sa_t, sb_t, lbf, lbb)


def _attn_kernel(lam_ref, q_ref, k_ref, v_ref, w_ref, o_ref, m_s, acc_s):
    n_chunks = k_ref.shape[0] // ATT_TK
    q = q_ref[...]
    lane = lax.broadcasted_iota(jnp.int32, q.shape, 1)
    zero = jnp.zeros_like(q)
    qz = (jnp.where(lane < HEAD_DIM, q, zero), jnp.where(lane >= HEAD_DIM, q, zero))
    m_s[...] = jnp.full(m_s.shape, -jnp.inf, F32)
    acc_s[...] = jnp.zeros(acc_s.shape, F32)

    def chunk(j):
        start = pl.multiple_of(j * ATT_TK, ATT_TK)
        kc = k_ref[pl.ds(start, ATT_TK), :]
        vc = v_ref[pl.ds(start, ATT_TK), :]
        for c in range(2):
            s = lax.dot_general(qz[c], kc, NT_DIMS, preferred_element_type=F32)
            m_prev = m_s[c]
            m_next = jnp.maximum(m_prev, jnp.max(s, axis=1, keepdims=True))
            alpha = jnp.exp2(m_prev - m_next)
            p = jnp.exp2(s - jnp.tile(m_next, (1, ATT_TK // LANES))).astype(BF16)
            acc_s[c] = acc_s[c] * jnp.tile(alpha, (1, 2)) + jnp.dot(p, vc, preferred_element_type=F32)
            m_s[c] = m_next

    unroll = math.gcd(n_chunks, ATT_UNROLL)

    def body(jj, carry):
        for u in range(unroll):
            chunk(jj * unroll + u)
        return carry

    lax.fori_loop(0, n_chunks // unroll, body, 0)

    lam = lam_ref[0]
    a1, a2 = acc_s[0], acc_s[1]
    o = a1[:, :ATT_V] / a1[:, ATT_V:] - lam * (a2[:, :ATT_V] / a2[:, ATT_V:])
    ms = jnp.mean(o * o, axis=-1, keepdims=True)
    o = o * lax.rsqrt(ms + EPS) * w_ref[...] * (1.0 - LAM_INIT)
    o_ref[...] = o.astype(o_ref.dtype)


def _attention(q, k, v_ext, lam, subln_w):
    B, S, _ = q.shape
    tq = min(1024, S)
    return pl.pallas_call(
        _attn_kernel,
        grid=(B, ATT_HEADS, S // tq),
        in_specs=[
            pl.BlockSpec(memory_space=pltpu.SMEM),
            pl.BlockSpec((None, tq, ATT_V), lambda b, h, i: (b, i, h)),
            pl.BlockSpec((None, S, ATT_V), lambda b, h, i: (b, 0, h)),
            pl.BlockSpec((None, S, 2 * ATT_V), lambda b, h, i: (b, 0, h)),
            pl.BlockSpec((1, ATT_V), lambda b, h, i: (0, 0)),
        ],
        out_specs=pl.BlockSpec((None, tq, ATT_V), lambda b, h, i: (b, i, h)),
        out_shape=jax.ShapeDtypeStruct((B, S, ATT_WIDTH), BF16),
        scratch_shapes=[
            pltpu.VMEM((2, tq, LANES), F32),
            pltpu.VMEM((2, tq, 2 * ATT_V), F32),
        ],
        compiler_params=_params("parallel", "parallel", "arbitrary"),
        name="diff_attention",
    )(lam, q, k, v_ext, subln_w)


def _hgrn_kernel(qf_ref, lff_ref, vf_ref, qb_ref, lfb_ref, vb_ref, lvlf_ref, lvlb_ref,
                 of_ref, ob_ref, stf_ref, stb_ref):
    @pl.when(pl.program_id(1) == 0)
    def _():
        stf_ref[...] = jnp.zeros(stf_ref.shape, F32)
        stb_ref[...] = jnp.zeros(stb_ref.shape, F32)

    _hgrn_chunk(qf_ref, lff_ref, vf_ref, lvlf_ref, of_ref, stf_ref, reverse=False)
    _hgrn_chunk(qb_ref, lfb_ref, vb_ref, lvlb_ref, ob_ref, stb_ref, reverse=True)


def _hgrn_chunk(q_ref, lf_ref, v_ref, lvl_ref, o_ref, st_ref, *, reverse):
    C = q_ref.shape[0]

    lf = lf_ref[...]
    q = q_ref[...].astype(F32)
    v = v_ref[...]
    kk = 1.0 - jnp.exp(lf)
    row = lax.broadcasted_iota(jnp.int32, lf.shape, 0)
    hi_side, lo_side = (kk, q) if reverse else (q, kk)

    def halves(x, m):
        lo = [x[b:b + m] for b in range(0, C, 2 * m)]
        up = [x[b + m:b + 2 * m] for b in range(0, C, 2 * m)]
        return jnp.concatenate(lo, axis=0), jnp.concatenate(up, axis=0)

    def merge(lo, up, m):
        parts = []
        for b in range(0, C // 2, m):
            parts += [lo[b:b + m], up[b:b + m]]
        return jnp.concatenate(parts, axis=0)

    pref, tot = lf, lf
    xs = []
    m = 1
    while m < SUBLANES:
        upper = (row & m) != 0
        if reverse:
            g = jnp.where(upper, pref - lf, tot - pref + lf)
        else:
            g = jnp.where(upper, pref, tot - pref)
        xs.append((jnp.where(upper, hi_side, lo_side) * jnp.exp(g)).astype(BF16))
        below = pltpu.roll(tot, m, 0)
        above = pltpu.roll(tot, C - m, 0)
        pref = jnp.where(upper, pref + below, pref)
        tot = tot + jnp.where(upper, below, above)
        m *= 2
    while m < C:
        p_lo, p_up = halves(pref, m)
        t_lo, t_up = halves(tot, m)
        hi_up = halves(hi_side, m)[1]
        lo_lo = halves(lo_side, m)[0]
        if reverse:
            lf_lo, lf_up = halves(lf, m)
            g_up, g_lo = p_up - lf_up, t_lo - p_lo + lf_lo
        else:
            g_up, g_lo = p_up, t_lo - p_lo
        xs.append(merge(lo_lo * jnp.exp(g_lo), hi_up * jnp.exp(g_up), m).astype(BF16))
        t_new = t_lo + t_up
        pref = merge(p_lo, p_up + t_lo, m)
        tot = merge(t_new, t_new, m)
        m *= 2

    if reverse:
        gq, gk = tot - pref + lf, pref - lf
    else:
        gq, gk = pref, tot - pref
    xq = (q * jnp.exp(gq)).astype(BF16)
    xk = (kk * jnp.exp(gk)).astype(BF16)
    chunk_decay = jnp.exp(tot[0:1, :])
    qk = q * kk
    lvl = lvl_ref[...]

    for h in range(HG_HEADS):
        sl = slice(h * HG_K, (h + 1) * HG_K)
        a = jnp.zeros((C, C), F32)
        for li, x in enumerate(xs):
            xh = x[:, sl]
            am = lax.dot_general(xh, xh, NT_DIMS, preferred_element_type=F32)
            a = jnp.where(lvl == li + 1, am, a)
        vh = v[:, sl]
        st = st_ref[h]
        o = jnp.dot(a.astype(BF16), vh, preferred_element_type=F32)
        o += lax.dot_general(xq[:, sl], st.astype(BF16), NT_DIMS, preferred_element_type=F32)
        o += jnp.sum(qk[:, sl], axis=1, keepdims=True) * vh.astype(F32)
        o_ref[:, sl] = o
        st_ref[h] = st * chunk_decay[:, sl] + lax.dot_general(
            vh, xk[:, sl], TN_DIMS, preferred_element_type=F32)


def _level_table(reverse):
    t = jnp.arange(HG_CHUNK, dtype=jnp.int32)[:, None]
    s = jnp.arange(HG_CHUNK, dtype=jnp.int32)[None, :]
    x = t ^ s
    lvl = jnp.zeros_like(x)
    for b in range(HG_CHUNK.bit_length() - 1):
        lvl = jnp.where(x >= (1 << b), b + 1, lvl)
    active = (s > t) if reverse else (s < t)
    return jnp.where(active, lvl, 0)


def _hgrn(qh, lf_f, lf_b, vh):
    B, S, _ = qh.shape
    C = HG_CHUNK
    nc = S // C
    fwd = pl.BlockSpec((None, C, HG_WIDTH), lambda b, c: (b, c, 0))
    bwd = pl.BlockSpec((None, C, HG_WIDTH), lambda b, c: (b, nc - 1 - c, 0))
    lvl = pl.BlockSpec((C, C), lambda b, c: (0, 0))
    out = jax.ShapeDtypeStruct((B, S, HG_WIDTH), F32)
    state = pltpu.VMEM((HG_HEADS, HG_K, HG_K), F32)
    return pl.pallas_call(
        _hgrn_kernel,
        grid=(B, nc),
        in_specs=[fwd, fwd, fwd, bwd, bwd, bwd, lvl, lvl],
        out_specs=[fwd, bwd],
        out_shape=[out, out],
        scratch_shapes=[state, state],
        compiler_params=_params("parallel", "arbitrary"),
        name="hgrn",
    )(qh, lf_f, vh, qh, lf_b, vh, _level_table(False), _level_table(True))


def _outproj_kernel(x_ref, oa_ref, of_ref, ob_ref, g_ref, hn_ref, wo_ref, fn_ref, wr_ref, br_ref,
                    x1_ref, h2_ref, srow_ref, gate_ref, cnt_ref):
    tm = x_ref.shape[0]

    oh = of_ref[...] + ob_ref[...]
    ms = jnp.mean(oh * oh, axis=-1, keepdims=True)
    ohn = oh * lax.rsqrt(ms + EPS) * hn_ref[...] * g_ref[...].astype(F32)
    mix = jnp.dot(oa_ref[...], wo_ref[:ATT_WIDTH, :], preferred_element_type=F32)
    mix += jnp.dot(ohn.astype(BF16), wo_ref[ATT_WIDTH:, :], preferred_element_type=F32)
    x1 = x_ref[...] + mix
    x1_ref[...] = x1
    ms = jnp.mean(x1 * x1, axis=-1, keepdims=True)
    h2 = x1 * lax.rsqrt(ms + EPS) * fn_ref[...]
    h2_ref[...] = h2.astype(BF16)

    logits = jnp.dot(h2, wr_ref[...], preferred_element_type=F32,
                     precision=lax.Precision.HIGHEST) + br_ref[...]
    lane = lax.broadcasted_iota(jnp.int32, logits.shape, 1)
    lane_f = lane.astype(F32)
    vals, idxs = [], []
    for _ in range(TOP_K):
        mx = jnp.max(logits, axis=1, keepdims=True)
        ix = jnp.min(jnp.where(logits == mx, lane_f, float(LANES)), axis=1, keepdims=True)
        vals.append(mx)
        idxs.append(ix)
        logits = jnp.where(lane_f == ix, -jnp.inf, logits)
    es = [jnp.exp(vv - vals[0]) for vv in vals]
    den = es[0] + es[1] + es[2] + es[3]

    onehots = [(lane_f == ix).astype(F32) for ix in idxs]
    multi = onehots[0] + onehots[1] + onehots[2] + onehots[3]
    r_i = lax.broadcasted_iota(jnp.int32, (tm, tm), 0)
    c_i = lax.broadcasted_iota(jnp.int32, (tm, tm), 1)
    tri = jnp.where(c_i < r_i, 1.0, 0.0).astype(BF16)
    before = jnp.dot(tri, multi.astype(BF16), preferred_element_type=F32)

    cnt = jnp.broadcast_to(jnp.sum(multi, axis=0, keepdims=True), cnt_ref.shape)
    run_len = jnp.ceil(cnt * (1.0 / RUN_ALIGN)) * RUN_ALIGN
    e_i = lax.broadcasted_iota(jnp.int32, (LANES, LANES), 0)
    e_j = lax.broadcasted_iota(jnp.int32, (LANES, LANES), 1)
    lower = jnp.where(e_i < e_j, 1.0, 0.0)
    stage_off = jnp.dot(run_len, lower, preferred_element_type=F32,
                        precision=lax.Precision.HIGHEST)[0:1, :]
    srow = jnp.full(logits.shape, -1.0, F32)
    gates = jnp.zeros(logits.shape, F32)
    for kslot in range(TOP_K):
        here = lane == kslot
        gates = jnp.where(here, es[kslot] / den, gates)
        row_k = jnp.sum(onehots[kslot] * (before + stage_off), axis=1, keepdims=True)
        srow = jnp.where(here, row_k, srow)
    srow_ref[...] = srow
    gate_ref[...] = gates
    cnt_ref[...] = cnt


def _outproj(x2d, oa, o_f, o_b, g, hgrn_norm, w_out, norm_ffn, w_router, b_router):
    T = x2d.shape[0]
    tm = _moe_tile(T)
    row = lambda i: (i, 0)
    fixed = lambda i: (0, 0)
    return pl.pallas_call(
        _outproj_kernel,
        grid=(T // tm,),
        in_specs=[
            pl.BlockSpec((tm, D_MODEL), row),
            pl.BlockSpec((tm, ATT_WIDTH), row),
            pl.BlockSpec((tm, HG_WIDTH), row),
            pl.BlockSpec((tm, HG_WIDTH), row),
            pl.BlockSpec((tm, HG_WIDTH), row),
            pl.BlockSpec((1, HG_WIDTH), fixed),
            pl.BlockSpec((D_MODEL, D_MODEL), fixed),
            pl.BlockSpec((1, D_MODEL), fixed),
            pl.BlockSpec((D_MODEL, LANES), fixed),
            pl.BlockSpec((1, LANES), fixed),
        ],
        out_specs=[
            pl.BlockSpec((tm, D_MODEL), row),
            pl.BlockSpec((tm, D_MODEL), row),
            pl.BlockSpec((tm, LANES), row),
            pl.BlockSpec((tm, LANES), row),
            pl.BlockSpec((None, SUBLANES, LANES), lambda i: (i, 0, 0)),
        ],
        out_shape=[
            jax.ShapeDtypeStruct((T, D_MODEL), F32),
            jax.ShapeDtypeStruct((T, D_MODEL), BF16),
            jax.ShapeDtypeStruct((T, LANES), F32),
            jax.ShapeDtypeStruct((T, LANES), F32),
            jax.ShapeDtypeStruct((T // tm, SUBLANES, LANES), F32),
        ],
        compiler_params=_params("parallel"),
        name="outproj_router",
    )(x2d, oa, o_f, o_b, g, hgrn_norm, w_out, norm_ffn, w_router, b_router)


def _moe_tile(T):
    return min(512, T)


def _pack_pairs(x):
    hi = lax.bitcast_convert_type(x[:, :PACK_W], U32)
    lo = lax.bitcast_convert_type(x[:, PACK_W:], U32)
    return (hi & jnp.uint32(0xFFFF0000)) | (lo >> 16)


def _unpack_pairs(u):
    hi = lax.bitcast_convert_type(u & jnp.uint32(0xFFFF0000), F32).astype(BF16)
    lo = lax.bitcast_convert_type(u << 16, F32).astype(BF16)
    return jnp.concatenate([hi, lo], axis=1)


def _stage_rows(tm):
    return pl.cdiv(TOP_K * tm + N_EXPERTS * RUN_ALIGN, STAGE_BLK) * STAGE_BLK


def _run_dmas(tile, run_start_ref, run_len_ref, stage_off_ref, make_copy, act):
    def per_expert(e, carry):
        n = run_len_ref[tile * N_EXPERTS + e]
        src = stage_off_ref[tile * N_EXPERTS + e]
        dst = run_start_ref[tile * N_EXPERTS + e]
        done = 0
        for size in DMA_ROWS:
            cnt = (n - done) // size

            def one(j, c, size=size, done=done):
                off = pl.multiple_of(done + j * size, RUN_ALIGN)
                act(make_copy(pl.multiple_of(src + off, RUN_ALIGN),
                              pl.multiple_of(dst + off, RUN_ALIGN), size))
                return c

            lax.fori_loop(0, cnt, one, 0)
            done = done + cnt * size
        return carry

    lax.fori_loop(0, N_EXPERTS, per_expert, 0)


def _dispatch_kernel(run_start_ref, run_len_ref, stage_off_ref, pad_start_ref, pad_len_ref, nu_ref,
                     h2_ref, srow_ref, xs_ref, stage, zero_s, sems, zsem):
    i = pl.program_id(0)
    nt = pl.num_programs(0)
    slot = i % 2
    tm = h2_ref.shape[0]
    sr = stage.shape[1]

    def copy_out(s):
        def make(stage_row, hbm_row, size):
            return pltpu.make_async_copy(stage.at[s, pl.ds(stage_row, size)],
                                         xs_ref.at[pl.ds(hbm_row, size)], sems.at[s])
        return make

    tables = (run_start_ref, run_len_ref, stage_off_ref)

    @pl.when(i >= 2)
    def _():
        _run_dmas(i - 2, *tables, copy_out(slot), lambda c: c.wait())

    @pl.when(i == 0)
    def _():
        zero_s[...] = jnp.zeros(zero_s.shape, U32)

        def zero_copy(hbm_row, size):
            return pltpu.make_async_copy(zero_s.at[pl.ds(0, size)], xs_ref.at[pl.ds(hbm_row, size)], zsem)

        for e in range(N_EXPERTS):
            done = 0
            for size in DMA_ROWS:
                cnt = (pad_len_ref[e] - done) // size

                def fill(j, c, e=e, size=size, done=done):
                    zero_copy(pl.multiple_of(pad_start_ref[e] + done + j * size, RUN_ALIGN), size).start()
                    return c

                def drain(j, c, e=e, size=size, done=done):
                    zero_copy(pl.multiple_of(pad_start_ref[e] + done + j * size, RUN_ALIGN), size).wait()
                    return c

                lax.fori_loop(0, cnt, fill, 0)
                lax.fori_loop(0, cnt, drain, 0)
                done = done + cnt * size

        per_blk = MOE_ROWS // DMA_ROWS[0]

        def fill_tail(j, c):
            zero_copy(pl.multiple_of(j * DMA_ROWS[0], RUN_ALIGN), DMA_ROWS[0]).start()
            return c

        def drain_tail(j, c):
            zero_copy(pl.multiple_of(j * DMA_ROWS[0], RUN_ALIGN), DMA_ROWS[0]).wait()
            return c

        lo, hi = nu_ref[0] * per_blk, (xs_ref.shape[0] // MOE_ROWS) * per_blk
        lax.fori_loop(lo, hi, fill_tail, 0)
        lax.fori_loop(lo, hi, drain_tail, 0)

    srow_t = srow_ref[...].T.astype(jnp.int32)
    h2 = h2_ref[...]
    r_iota = lax.broadcasted_iota(jnp.int32, (STAGE_BLK, tm), 0)
    for b in range(sr // STAGE_BLK):
        hit = r_iota == srow_t[0:1, :] - b * STAGE_BLK
        for kslot in range(1, TOP_K):
            hit = hit | (r_iota == srow_t[kslot:kslot + 1, :] - b * STAGE_BLK)
        onehot = jnp.where(hit, 1.0, 0.0).astype(BF16)
        stage[slot, b * STAGE_BLK:(b + 1) * STAGE_BLK, :] = _pack_pairs(
            jnp.dot(onehot, h2, preferred_element_type=F32))
    _run_dmas(i, *tables, copy_out(slot), lambda c: c.start())

    @pl.when(i == nt - 1)
    def _():
        @pl.when(nt >= 2)
        def _():
            _run_dmas(i - 1, *tables, copy_out(1 - slot), lambda c: c.wait())

        _run_dmas(i, *tables, copy_out(slot), lambda c: c.wait())


def _dispatch(h2, srow, run_start, run_len, stage_off, pad_start, pad_len, n_used, n_rows):
    T = h2.shape[0]
    tm = _moe_tile(T)
    row = lambda i, *_: (i, 0)
    return pl.pallas_call(
        _dispatch_kernel,
        grid_spec=pltpu.PrefetchScalarGridSpec(
            num_scalar_prefetch=6,
            grid=(T // tm,),
            in_specs=[
                pl.BlockSpec((tm, D_MODEL), row),
                pl.BlockSpec((tm, LANES), row),
            ],
            out_specs=pl.BlockSpec(memory_space=pl.ANY),
            scratch_shapes=[
                pltpu.VMEM((2, _stage_rows(tm), PACK_W), U32),
                pltpu.VMEM((DMA_ROWS[0], PACK_W), U32),
                pltpu.SemaphoreType.DMA((2,)),
                pltpu.SemaphoreType.DMA(()),
            ],
        ),
        out_shape=jax.ShapeDtypeStruct((n_rows, PACK_W), U32),
        compiler_params=_params("arbitrary"),
        name="moe_dispatch",
    )(run_start, run_len, stage_off, pad_start, pad_len, n_used, h2, srow)


def _expert_kernel(be_ref, nu_ref, xs_ref, wg_ref, wl_ref, bg_ref, bl_ref, wd_ref, bd_ref, y_ref):
    @pl.when(pl.program_id(0) < nu_ref[0])
    def _():
        x = _unpack_pairs(xs_ref[...])
        glu = jnp.dot(x, wg_ref[...], preferred_element_type=F32) + bg_ref[...]
        lin = jnp.dot(x, wl_ref[...], preferred_element_type=F32) + bl_ref[...]
        glu = jnp.minimum(glu, SWIGLU_LIMIT)
        lin = jnp.clip(lin, -SWIGLU_LIMIT, SWIGLU_LIMIT)
        act = glu * jax.nn.sigmoid(SWIGLU_ALPHA * glu) * (lin + 1.0)
        y = jnp.dot(act.astype(BF16), wd_ref[...], preferred_element_type=F32) + bd_ref[...]
        y_ref[...] = _pack_pairs(y.astype(BF16).astype(F32))

    @pl.when(pl.program_id(0) >= nu_ref[0])
    def _():
        y_ref[...] = jnp.zeros(y_ref.shape, U32)


def _experts(xs, blk_e, n_used, wg, wl, bg, bl, wd, bd):
    n_rows = xs.shape[0]
    nblk = n_rows // MOE_ROWS
    rows = lambda i, be, nu: (jnp.minimum(i, nu[0] - 1), 0)
    per_e = lambda i, be, nu: (be[jnp.minimum(i, nu[0] - 1)], 0, 0)
    return pl.pallas_call(
        _expert_kernel,
        grid_spec=pltpu.PrefetchScalarGridSpec(
            num_scalar_prefetch=2,
            grid=(nblk,),
            in_specs=[
                pl.BlockSpec((MOE_ROWS, PACK_W), rows),
                pl.BlockSpec((None, D_MODEL, D_FF), per_e),
                pl.BlockSpec((None, D_MODEL, D_FF), per_e),
                pl.BlockSpec((None, 1, D_FF), per_e),
                pl.BlockSpec((None, 1, D_FF), per_e),
                pl.BlockSpec((None, D_FF, D_MODEL), per_e),
                pl.BlockSpec((None, 1, D_MODEL), per_e),
            ],
            out_specs=pl.BlockSpec((MOE_ROWS, PACK_W), lambda i, be, nu: (i, 0)),
        ),
        out_shape=jax.ShapeDtypeStruct((n_rows, PACK_W), U32),
        compiler_params=_params("arbitrary"),
        name="moe_experts",
    )(blk_e, n_used, xs, wg, wl, bg, bl, wd, bd)


def _split_kernel(w_ref, perm_ref, g_ref, l_ref):
    w = w_ref[...].astype(BF16)
    r = jnp.dot(w, perm_ref[...], preferred_element_type=F32)
    g_ref[...] = r[:, :LANES].astype(BF16)
    l_ref[...] = r[:, LANES:].astype(BF16)


def _split_gate_up(w_gate_up):
    E, D, F2 = w_gate_up.shape
    grp = 2 * LANES
    src = jnp.arange(grp, dtype=jnp.int32)[:, None]
    dst = jnp.arange(grp, dtype=jnp.int32)[None, :]
    perm = (dst == (src // 2 + LANES * (src % 2))).astype(BF16)
    half = jax.ShapeDtypeStruct((E, D, F2 // 2), BF16)
    out_spec = pl.BlockSpec((None, D, LANES), lambda e, j: (e, 0, j))
    return pl.pallas_call(
        _split_kernel,
        grid=(E, F2 // grp),
        in_specs=[pl.BlockSpec((None, D, grp), lambda e, j: (e, 0, j)),
                  pl.BlockSpec((grp, grp), lambda e, j: (0, 0))],
        out_specs=[out_spec, out_spec],
        out_shape=[half, half],
        compiler_params=_params("parallel", "parallel"),
        name="split_gate_up",
    )(w_gate_up, perm)


def _combine_kernel(run_start_ref, run_len_ref, stage_off_ref,
                    y_ref, x1_ref, srow_ref, gate_ref, nf_ref, o_ref, stage, sems):
    i = pl.program_id(0)
    nt = pl.num_programs(0)
    slot = i % 2
    tm = x1_ref.shape[0]
    sr = stage.shape[1]

    def copy_in(s):
        def make(stage_row, hbm_row, size):
            return pltpu.make_async_copy(y_ref.at[pl.ds(hbm_row, size)],
                                         stage.at[s, pl.ds(stage_row, size)], sems.at[s])
        return make

    tables = (run_start_ref, run_len_ref, stage_off_ref)

    @pl.when(i == 0)
    def _():
        stage[...] = jnp.zeros(stage.shape, U32)
        _run_dmas(i, *tables, copy_in(0), lambda c: c.start())

    @pl.when(i + 1 < nt)
    def _():
        _run_dmas(i + 1, *tables, copy_in(1 - slot), lambda c: c.start())

    _run_dmas(i, *tables, copy_in(slot), lambda c: c.wait())

    srow = srow_ref[...]
    gates = gate_ref[...]
    lane = lax.broadcasted_iota(jnp.int32, gates.shape, 1)
    pick = lambda a, kslot: jnp.sum(jnp.where(lane == kslot, a, 0.0), axis=1, keepdims=True)
    cols = [pick(srow, kslot).astype(jnp.int32) for kslot in range(TOP_K)]
    g_cols = [pick(gates, kslot) for kslot in range(TOP_K)]
    r_iota = lax.broadcasted_iota(jnp.int32, (tm, STAGE_BLK), 1)
    blocks = []
    for b in range(sr // STAGE_BLK):
        w = jnp.zeros((tm, STAGE_BLK), F32)
        for c, g_k in zip(cols, g_cols):
            w = jnp.where(r_iota == c - b * STAGE_BLK, g_k, w)
        blocks.append(w.astype(BF16))
    weights = jnp.concatenate(blocks, axis=1)
    staged = _unpack_pairs(stage[slot])
    acc = x1_ref[...] + jnp.dot(weights, staged, preferred_element_type=F32)
    ms = jnp.mean(acc * acc, axis=-1, keepdims=True)
    o_ref[...] = acc * lax.rsqrt(ms + EPS) * nf_ref[...]


def _combine(y, x1, srow, gates, run_start, run_len, stage_off, norm_final):
    T = x1.shape[0]
    tm = _moe_tile(T)
    row = lambda i, *_: (i, 0)
    return pl.pallas_call(
        _combine_kernel,
        grid_spec=pltpu.PrefetchScalarGridSpec(
            num_scalar_prefetch=3,
            grid=(T // tm,),
            in_specs=[
                pl.BlockSpec(memory_space=pl.ANY),
                pl.BlockSpec((tm, D_MODEL), row),
                pl.BlockSpec((tm, LANES), row),
                pl.BlockSpec((tm, LANES), row),
                pl.BlockSpec((1, D_MODEL), lambda i, *_: (0, 0)),
            ],
            out_specs=pl.BlockSpec((tm, D_MODEL), row),
            scratch_shapes=[
                pltpu.VMEM((2, _stage_rows(tm), PACK_W), U32),
                pltpu.SemaphoreType.DMA((2,)),
            ],
        ),
        out_shape=jax.ShapeDtypeStruct((T, D_MODEL), F32),
        compiler_params=_params("arbitrary"),
        name="moe_combine",
    )(run_start, run_len, stage_off, y, x1, srow, gates, norm_final)


def _rope_tables(seq):
    half = ROT_DIM // 2
    inv = ROPE_THETA ** (-jnp.arange(0, ROT_DIM, 2, dtype=F32) / ROT_DIM)
    ang = jnp.arange(seq, dtype=F32)[:, None] * inv[None, :]
    cos, sin = jnp.cos(ang), jnp.sin(ang)
    pad = jnp.zeros((seq, HEAD_DIM - ROT_DIM), F32)
    zero = jnp.zeros((seq, half), F32)
    cos_t = jnp.concatenate([cos, cos, pad + 1.0], axis=1)
    sa_t = jnp.concatenate([-sin, zero, pad], axis=1)
    sb_t = jnp.concatenate([zero, sin, pad], axis=1)
    rep = LANES // HEAD_DIM
    return tuple(jnp.tile(t, (1, rep)) for t in (cos_t, sa_t, sb_t))


def _trunk(x, p):
    B, S, _ = x.shape
    T = B * S
    x2d = x.reshape(T, D_MODEL)
    q, k, v, qh, lff, lfb, ih, g = _inproj(x2d, S, p["norm_mix"], p["w_in"],
                                           *(t[:S] for t in p["rope"]), p["lb_fwd"], p["lb_bwd"])
    to3 = lambda a: a.reshape(B, S, a.shape[-1])
    oa = _attention(to3(q), to3(k), to3(v), p["lam"], p["subln_w"])
    o_f, o_b = _hgrn(to3(qh), to3(lff), to3(lfb), to3(ih))
    x1, h2, srow, gates, counts = _outproj(
        x2d, oa.reshape(T, ATT_WIDTH), o_f.reshape(T, HG_WIDTH), o_b.reshape(T, HG_WIDTH), g,
        p["hgrn_norm"], p["w_out"], p["norm_ffn"], p["w_router"], p["b_router"])

    tm = _moe_tile(T)
    nt = T // tm
    n_rows = (pl.cdiv(T * TOP_K + nt * N_EXPERTS * RUN_ALIGN, MOE_ROWS) + N_EXPERTS) * MOE_ROWS
    cnt = counts[:, 0, :N_EXPERTS].astype(jnp.int32)
    run_len = (cnt + RUN_ALIGN - 1) // RUN_ALIGN * RUN_ALIGN
    sizes = jnp.sum(run_len, axis=0)
    padded = (sizes + MOE_ROWS - 1) // MOE_ROWS * MOE_ROWS
    pends = jnp.cumsum(padded)
    pstart = pends - padded
    run_start = pstart[None, :] + jnp.cumsum(run_len, axis=0) - run_len
    stage_off = jnp.cumsum(run_len, axis=1) - run_len
    n_used = (pends[-1:] // MOE_ROWS).astype(jnp.int32)
    blk_row = jnp.arange(n_rows // MOE_ROWS, dtype=jnp.int32) * MOE_ROWS
    blk_e = jnp.minimum(jnp.sum(pends[None, :] <= blk_row[:, None], axis=1),
                        N_EXPERTS - 1).astype(jnp.int32)
    flat = lambda a: a.reshape(-1).astype(jnp.int32)

    xs = _dispatch(h2, srow, flat(run_start), flat(run_len), flat(stage_off),
                   flat(pstart + sizes), flat(padded - sizes), n_used, n_rows)
    y = _experts(xs, blk_e, n_used, p["wg"], p["wl"], p["bg"], p["bl"], p["wd"], p["bd"])
    out = _combine(y, x1, srow, gates, flat(run_start), flat(run_len), flat(stage_off),
                   p["norm_final"])
    return out.reshape(B, S, D_MODEL)


def kernel(x_prompt, x_sample, norm_mix, w_in, lambda_q1, lambda_k1, lambda_q2, lambda_k2, subln_w,
           lb_fwd, lb_bwd, hgrn_norm, w_out, norm_ffn, w_router, b_router, w_gate_up, b_gate_up,
           w_down, b_down, norm_final):
    l = 0
    lam = (jnp.exp(jnp.sum(lambda_q1[l] * lambda_k1[l])) - jnp.exp(jnp.sum(lambda_q2[l] * lambda_k2[l]))
           + LAM_INIT)
    wg, wl = _split_gate_up(w_gate_up[l])
    bgu = b_gate_up[l]
    p = {
        "norm_mix": norm_mix[l][None, :],
        "w_in": w_in[l].astype(BF16),
        "lam": lam.reshape(1).astype(F32),
        "subln_w": subln_w[l][None, :],
        "lb_fwd": jnp.cumsum(jax.nn.softmax(lb_fwd, axis=0), axis=0)[l][None, :],
        "lb_bwd": jnp.cumsum(jax.nn.softmax(lb_bwd, axis=0), axis=0)[l][None, :],
        "hgrn_norm": hgrn_norm[l][None, :],
        "w_out": w_out[l].astype(BF16),
        "norm_ffn": norm_ffn[l][None, :],
        "w_router": jnp.pad(w_router[l], ((0, 0), (0, LANES - N_EXPERTS))),
        "b_router": jnp.pad(b_router[l], (0, LANES - N_EXPERTS), constant_values=-jnp.inf)[None, :],
        "wg": wg,
        "wl": wl,
        "bg": bgu[:, None, 0::2],
        "bl": bgu[:, None, 1::2],
        "wd": w_down[l].astype(BF16),
        "bd": b_down[l][:, None, :],
        "norm_final": norm_final[None, :],
        "rope": _rope_tables(max(x_prompt.shape[1], x_sample.shape[1])),
    }
    return (_trunk(x_prompt, p), _trunk(x_sample, p))
```

```python
import math

import jax
import jax.numpy as jnp
from jax import lax
from jax.experimental import pallas as pl
from jax.experimental.pallas import tpu as pltpu

D_MODEL = 1024
HEAD_DIM = 64
ATT_HEADS = 4
ATT_V = 2 * HEAD_DIM
ATT_WIDTH = ATT_HEADS * ATT_V
ROT_DIM = HEAD_DIM // 4
ROPE_THETA = 500000.0
HG_HEADS = 4
HG_K = 128
HG_WIDTH = HG_HEADS * HG_K
IN_COLS = 4096
GROUP_COLS = 512
N_EXPERTS = 32
TOP_K = 4
D_FF = 1024
SWIGLU_LIMIT = 7.0
SWIGLU_ALPHA = 1.702
EPS = 1e-5
LAM_INIT = 0.8 - 0.6 * math.exp(-0.3 * 0)

LANES = 128
SUBLANES = 8
RUN_ALIGN = SUBLANES
STAGE_BLK = 256
HG_CHUNK = 128
ATT_TK = 256
ATT_UNROLL = 16
LOG2_E = math.log2(math.e)
MOE_ROWS = 512
VMEM_LIMIT = 56 * 1024 * 1024

F32 = jnp.float32
BF16 = jnp.bfloat16
U32 = jnp.uint32
PACK_W = D_MODEL // 2
NT_DIMS = (((1,), (1,)), ((), ()))
TN_DIMS = (((0,), (0,)), ((), ()))


def _params(*sem):
    return pltpu.CompilerParams(dimension_semantics=sem, vmem_limit_bytes=VMEM_LIMIT)


def _inproj_kernel(x_ref, nw_ref, w_ref, cos_ref, sa_ref, sb_ref, lbf_ref, lbb_ref,
                   q_ref, k_ref, v_ref, qh_ref, lff_ref, lfb_ref, ih_ref, g_ref):
    x = x_ref[...]
    ms = jnp.mean(x * x, axis=-1, keepdims=True)
    h = (x * lax.rsqrt(ms + EPS) * nw_ref[...]).astype(BF16)

    def proj(c):
        return jnp.dot(h, w_ref[:, c * GROUP_COLS:(c + 1) * GROUP_COLS],
                       preferred_element_type=F32)

    cos, sa, sb = cos_ref[...], sa_ref[...], sb_ref[...]

    def rope(p):
        outs = []
        for j in range(GROUP_COLS // LANES):
            xx = p[:, j * LANES:(j + 1) * LANES]
            outs.append(xx * cos + pltpu.roll(xx, LANES - ROT_DIM // 2, 1) * sa
                        + pltpu.roll(xx, ROT_DIM // 2, 1) * sb)
        return jnp.concatenate(outs, axis=1)

    def log_forget(p, lb):
        return jnp.log(lb + (1.0 - lb) * jax.nn.sigmoid(p))

    q_ref[...] = (rope(proj(0)) * (HEAD_DIM ** -0.5 * LOG2_E)).astype(BF16)
    k_ref[...] = rope(proj(1)).astype(BF16)
    va = proj(2).astype(BF16)
    ones = jnp.ones((va.shape[0], ATT_V), BF16)
    for hh in range(ATT_HEADS):
        v_ref[:, 2 * hh * ATT_V:(2 * hh + 1) * ATT_V] = va[:, hh * ATT_V:(hh + 1) * ATT_V]
        v_ref[:, (2 * hh + 1) * ATT_V:(2 * hh + 2) * ATT_V] = ones
    qh_ref[...] = proj(3).astype(BF16)
    lff_ref[...] = log_forget(proj(4), lbf_ref[...])
    lfb_ref[...] = log_forget(proj(5), lbb_ref[...])
    ih_ref[...] = proj(6).astype(BF16)
    gh = proj(7)
    g_ref[...] = (gh * jax.nn.sigmoid(gh)).astype(BF16)


def _inproj(x2d, seq, norm_w, w_in, cos_t, sa_t, sb_t, lbf, lbb):
    T = x2d.shape[0]
    tm = min(512, seq)
    nseq = seq // tm
    row = lambda i: (i, 0)
    fixed = lambda i: (0, 0)
    tab = lambda i: (i % nseq, 0)
    out_dt = [BF16, BF16, BF16, BF16, F32, F32, BF16, BF16]
    out_w = [GROUP_COLS, GROUP_COLS, 2 * ATT_WIDTH] + [GROUP_COLS] * 5
    return pl.pallas_call(
        _inproj_kernel,
        grid=(T // tm,),
        in_specs=[
            pl.BlockSpec((tm, D_MODEL), row),
            pl.BlockSpec((1, D_MODEL), fixed),
            pl.BlockSpec((D_MODEL, IN_COLS), fixed),
            pl.BlockSpec((tm, LANES), tab),
            pl.BlockSpec((tm, LANES), tab),
            pl.BlockSpec((tm, LANES), tab),
            pl.BlockSpec((1, GROUP_COLS), fixed),
            pl.BlockSpec((1, GROUP_COLS), fixed),
        ],
        out_specs=[pl.BlockSpec((tm, w), row) for w in out_w],
        out_shape=[jax.ShapeDtypeStruct((T, w), dt) for w, dt in zip(out_w, out_dt)],
        compiler_params=_params("parallel"),
        name="inproj",
    )(x2d, norm_w, w_in, cos_t, sa_t, sb_t, lbf, lbb)


def _attn_kernel(lam_ref, q_ref, k_ref, v_ref, w_ref, o_ref, m_s, acc_s):
    n_chunks = k_ref.shape[0] // ATT_TK
    q = q_ref[...]
    lane = lax.broadcasted_iota(jnp.int32, q.shape, 1)
    zero = jnp.zeros_like(q)
    qz = (jnp.where(lane < HEAD_DIM, q, zero), jnp.where(lane >= HEAD_DIM, q, zero))
    m_s[...] = jnp.full(m_s.shape, -jnp.inf, F32)
    acc_s[...] = jnp.zeros(acc_s.shape, F32)

    def chunk(j):
        start = pl.multiple_of(j * ATT_TK, ATT_TK)
        kc = k_ref[pl.ds(start, ATT_TK), :]
        vc = v_ref[pl.ds(start, ATT_TK), :]
        for c in range(2):
            s = lax.dot_general(qz[c], kc, NT_DIMS, preferred_element_type=F32)
            m_prev = m_s[c]
            m_next = jnp.maximum(m_prev, jnp.max(s, axis=1, keepdims=True))
            alpha = jnp.exp2(m_prev - m_next)
            p = jnp.exp2(s - jnp.tile(m_next, (1, ATT_TK // LANES))).astype(BF16)
            acc_s[c] = acc_s[c] * jnp.tile(alpha, (1, 2)) + jnp.dot(p, vc, preferred_element_type=F32)
            m_s[c] = m_next

    unroll = math.gcd(n_chunks, ATT_UNROLL)

    def body(jj, carry):
        for u in range(unroll):
            chunk(jj * unroll + u)
        return carry

    lax.fori_loop(0, n_chunks // unroll, body, 0)

    lam = lam_ref[0]
    a1, a2 = acc_s[0], acc_s[1]
    o = a1[:, :ATT_V] / a1[:, ATT_V:] - lam * (a2[:, :ATT_V] / a2[:, ATT_V:])
    ms = jnp.mean(o * o, axis=-1, keepdims=True)
    o = o * lax.rsqrt(ms + EPS) * w_ref[...] * (1.0 - LAM_INIT)
    o_ref[...] = o.astype(o_ref.dtype)


def _attention(q, k, v_ext, lam, subln_w):
    B, S, _ = q.shape
    tq = min(1024, S)
    return pl.pallas_call(
        _attn_kernel,
        grid=(B, ATT_HEADS, S // tq),
        in_specs=[
            pl.BlockSpec(memory_space=pltpu.SMEM),
            pl.BlockSpec((None, tq, ATT_V), lambda b, h, i: (b, i, h)),
            pl.BlockSpec((None, S, ATT_V), lambda b, h, i: (b, 0, h)),
            pl.BlockSpec((None, S, 2 * ATT_V), lambda b, h, i: (b, 0, h)),
            pl.BlockSpec((1, ATT_V), lambda b, h, i: (0, 0)),
        ],
        out_specs=pl.BlockSpec((None, tq, ATT_V), lambda b, h, i: (b, i, h)),
        out_shape=jax.ShapeDtypeStruct((B, S, ATT_WIDTH), BF16),
        scratch_shapes=[
            pltpu.VMEM((2, tq, LANES), F32),
            pltpu.VMEM((2, tq, 2 * ATT_V), F32),
        ],
        compiler_params=_params("parallel", "parallel", "arbitrary"),
        name="diff_attention",
    )(lam, q, k, v_ext, subln_w)


def _hgrn_kernel(qf_ref, lff_ref, vf_ref, qb_ref, lfb_ref, vb_ref, lvlf_ref, lvlb_ref,
                 of_ref, ob_ref, stf_ref, stb_ref):
    @pl.when(pl.program_id(1) == 0)
    def _():
        stf_ref[...] = jnp.zeros(stf_ref.shape, F32)
        stb_ref[...] = jnp.zeros(stb_ref.shape, F32)

    _hgrn_chunk(qf_ref, lff_ref, vf_ref, lvlf_ref, of_ref, stf_ref, reverse=False)
    _hgrn_chunk(qb_ref, lfb_ref, vb_ref, lvlb_ref, ob_ref, stb_ref, reverse=True)


def _hgrn_chunk(q_ref, lf_ref, v_ref, lvl_ref, o_ref, st_ref, *, reverse):
    C = q_ref.shape[0]

    lf = lf_ref[...]
    q = q_ref[...].astype(F32)
    v = v_ref[...]
    kk = 1.0 - jnp.exp(lf)
    row = lax.broadcasted_iota(jnp.int32, lf.shape, 0)
    hi_side, lo_side = (kk, q) if reverse else (q, kk)

    def halves(x, m):
        lo = [x[b:b + m] for b in range(0, C, 2 * m)]
        up = [x[b + m:b + 2 * m] for b in range(0, C, 2 * m)]
        return jnp.concatenate(lo, axis=0), jnp.concatenate(up, axis=0)

    def merge(lo, up, m):
        parts = []
        for b in range(0, C // 2, m):
            parts += [lo[b:b + m], up[b:b + m]]
        return jnp.concatenate(parts, axis=0)

    pref, tot = lf, lf
    xs = []
    m = 1
    while m < SUBLANES:
        upper = (row & m) != 0
        if reverse:
            g = jnp.where(upper, pref - lf, tot - pref + lf)
        else:
            g = jnp.where(upper, pref, tot - pref)
        xs.append((jnp.where(upper, hi_side, lo_side) * jnp.exp(g)).astype(BF16))
        grouped = tot.reshape(C // SUBLANES, SUBLANES, tot.shape[1])
        below = pltpu.roll(grouped, m, 1).reshape(tot.shape)
        above = pltpu.roll(grouped, SUBLANES - m, 1).reshape(tot.shape)
        pref = jnp.where(upper, pref + below, pref)
        tot = tot + jnp.where(upper, below, above)
        m *= 2
    while m < C:
        p_lo, p_up = halves(pref, m)
        t_lo, t_up = halves(tot, m)
        hi_up = halves(hi_side, m)[1]
        lo_lo = halves(lo_side, m)[0]
        if reverse:
            lf_lo, lf_up = halves(lf, m)
            g_up, g_lo = p_up - lf_up, t_lo - p_lo + lf_lo
        else:
            g_up, g_lo = p_up, t_lo - p_lo
        xs.append(merge(lo_lo * jnp.exp(g_lo), hi_up * jnp.exp(g_up), m).astype(BF16))
        t_new = t_lo + t_up
        pref = merge(p_lo, p_up + t_lo, m)
        tot = merge(t_new, t_new, m)
        m *= 2

    if reverse:
        gq, gk = tot - pref + lf, pref - lf
    else:
        gq, gk = pref, tot - pref
    xq = (q * jnp.exp(gq)).astype(BF16)
    xk = (kk * jnp.exp(gk)).astype(BF16)
    chunk_decay = jnp.exp(tot[0:1, :])
    qk = q * kk
    lvl = lvl_ref[...]
    at_level = [lvl == li + 1 for li in range(len(xs))]

    for h in range(HG_HEADS):
        sl = slice(h * HG_K, (h + 1) * HG_K)
        a = jnp.zeros((C, C), F32)
        for li, x in enumerate(xs):
            xh = x[:, sl]
            am = lax.dot_general(xh, xh, NT_DIMS, preferred_element_type=F32)
            a = jnp.where(at_level[li], am, a)
        vh = v[:, sl]
        st = st_ref[h]
        o = jnp.dot(a.astype(BF16), vh, preferred_element_type=F32)
        o += lax.dot_general(xq[:, sl], st.astype(BF16), NT_DIMS, preferred_element_type=F32)
        o += jnp.sum(qk[:, sl], axis=1, keepdims=True) * vh.astype(F32)
        o_ref[:, sl] = o
        st_ref[h] = st * chunk_decay[:, sl] + lax.dot_general(
            vh, xk[:, sl], TN_DIMS, preferred_element_type=F32)


def _level_table(reverse):
    t = jnp.arange(HG_CHUNK, dtype=jnp.int32)[:, None]
    s = jnp.arange(HG_CHUNK, dtype=jnp.int32)[None, :]
    x = t ^ s
    lvl = jnp.zeros_like(x)
    for b in range(HG_CHUNK.bit_length() - 1):
        lvl = jnp.where(x >= (1 << b), b + 1, lvl)
    active = (s > t) if reverse else (s < t)
    return jnp.where(active, lvl, 0)


def _hgrn(qh, lf_f, lf_b, vh):
    B, S, _ = qh.shape
    C = HG_CHUNK
    nc = S // C
    fwd = pl.BlockSpec((None, C, HG_WIDTH), lambda b, c: (b, c, 0))
    bwd = pl.BlockSpec((None, C, HG_WIDTH), lambda b, c: (b, nc - 1 - c, 0))
    lvl = pl.BlockSpec((C, C), lambda b, c: (0, 0))
    out = jax.ShapeDtypeStruct((B, S, HG_WIDTH), F32)
    state = pltpu.VMEM((HG_HEADS, HG_K, HG_K), F32)
    return pl.pallas_call(
        _hgrn_kernel,
        grid=(B, nc),
        in_specs=[fwd, fwd, fwd, bwd, bwd, bwd, lvl, lvl],
        out_specs=[fwd, bwd],
        out_shape=[out, out],
        scratch_shapes=[state, state],
        compiler_params=_params("parallel", "arbitrary"),
        name="hgrn",
    )(qh, lf_f, vh, qh, lf_b, vh, _level_table(False), _level_table(True))


def _outproj_kernel(x_ref, oa_ref, of_ref, ob_ref, g_ref, hn_ref, wo_ref, fn_ref, wr_ref, br_ref,
                    x1_ref, h2_ref, srow_ref, gate_ref, cnt_ref):
    tm = x_ref.shape[0]

    oh = of_ref[...] + ob_ref[...]
    ms = jnp.mean(oh * oh, axis=-1, keepdims=True)
    ohn = oh * lax.rsqrt(ms + EPS) * hn_ref[...] * g_ref[...].astype(F32)
    mix = jnp.dot(oa_ref[...], wo_ref[:ATT_WIDTH, :], preferred_element_type=F32)
    mix += jnp.dot(ohn.astype(BF16), wo_ref[ATT_WIDTH:, :], preferred_element_type=F32)
    x1 = x_ref[...] + mix
    x1_ref[...] = x1
    ms = jnp.mean(x1 * x1, axis=-1, keepdims=True)
    h2 = x1 * lax.rsqrt(ms + EPS) * fn_ref[...]
    h2_ref[...] = h2.astype(BF16)

    wr = wr_ref[...]
    w_hi = wr.astype(BF16)
    w_lo = (wr - w_hi.astype(F32)).astype(BF16)
    h_hi = h2.astype(BF16)
    h_lo = (h2 - h_hi.astype(F32)).astype(BF16)
    logits = (jnp.dot(h_hi, w_hi, preferred_element_type=F32)
              + jnp.dot(h_hi, w_lo, preferred_element_type=F32)
              + jnp.dot(h_lo, w_hi, preferred_element_type=F32)) + br_ref[...]
    lane = lax.broadcasted_iota(jnp.int32, logits.shape, 1)
    lane_f = lane.astype(F32)
    vals, idxs = [], []
    for _ in range(TOP_K):
        mx = jnp.max(logits, axis=1, keepdims=True)
        ix = jnp.min(jnp.where(logits == mx, lane_f, float(LANES)), axis=1, keepdims=True)
        vals.append(mx)
        idxs.append(ix)
        logits = jnp.where(lane_f == ix, -jnp.inf, logits)
    es = [jnp.exp(vv - vals[0]) for vv in vals]
    den = es[0] + es[1] + es[2] + es[3]

    onehots = [(lane_f == ix).astype(F32) for ix in idxs]
    multi = onehots[0] + onehots[1] + onehots[2] + onehots[3]
    r_i = lax.broadcasted_iota(jnp.int32, (tm, tm), 0)
    c_i = lax.broadcasted_iota(jnp.int32, (tm, tm), 1)
    tri = jnp.where(c_i < r_i, 1.0, 0.0).astype(BF16)
    before = jnp.dot(tri, multi.astype(BF16), preferred_element_type=F32)

    cnt = jnp.broadcast_to(jnp.sum(multi, axis=0, keepdims=True), cnt_ref.shape)
    run_len = jnp.ceil(cnt * (1.0 / RUN_ALIGN)) * RUN_ALIGN
    e_i = lax.broadcasted_iota(jnp.int32, (LANES, LANES), 0)
    e_j = lax.broadcasted_iota(jnp.int32, (LANES, LANES), 1)
    lower = jnp.where(e_i < e_j, 1.0, 0.0)
    stage_off = jnp.dot(run_len, lower, preferred_element_type=F32,
                        precision=lax.Precision.HIGHEST)[0:1, :]
    srow = jnp.full(logits.shape, -1.0, F32)
    gates = jnp.zeros(logits.shape, F32)
    for kslot in range(TOP_K):
        here = lane == kslot
        gates = jnp.where(here, es[kslot] / den, gates)
        row_k = jnp.sum(onehots[kslot] * (before + stage_off), axis=1, keepdims=True)
        srow = jnp.where(here, row_k, srow)
    srow_ref[...] = srow
    gate_ref[...] = gates
    cnt_ref[...] = cnt


def _outproj(x2d, oa, o_f, o_b, g, hgrn_norm, w_out, norm_ffn, w_router, b_router):
    T = x2d.shape[0]
    tm = _moe_tile(T)
    row = lambda i: (i, 0)
    fixed = lambda i: (0, 0)
    return pl.pallas_call(
        _outproj_kernel,
        grid=(T // tm,),
        in_specs=[
            pl.BlockSpec((tm, D_MODEL), row),
            pl.BlockSpec((tm, ATT_WIDTH), row),
            pl.BlockSpec((tm, HG_WIDTH), row),
            pl.BlockSpec((tm, HG_WIDTH), row),
            pl.BlockSpec((tm, HG_WIDTH), row),
            pl.BlockSpec((1, HG_WIDTH), fixed),
            pl.BlockSpec((D_MODEL, D_MODEL), fixed),
            pl.BlockSpec((1, D_MODEL), fixed),
            pl.BlockSpec((D_MODEL, LANES), fixed),
            pl.BlockSpec((1, LANES), fixed),
        ],
        out_specs=[
            pl.BlockSpec((tm, D_MODEL), row),
            pl.BlockSpec((tm, D_MODEL), row),
            pl.BlockSpec((tm, LANES), row),
            pl.BlockSpec((tm, LANES), row),
            pl.BlockSpec((None, SUBLANES, LANES), lambda i: (i, 0, 0)),
        ],
        out_shape=[
            jax.ShapeDtypeStruct((T, D_MODEL), F32),
            jax.ShapeDtypeStruct((T, D_MODEL), BF16),
            jax.ShapeDtypeStruct((T, LANES), F32),
            jax.ShapeDtypeStruct((T, LANES), F32),
            jax.ShapeDtypeStruct((T // tm, SUBLANES, LANES), F32),
        ],
        compiler_params=_params("parallel"),
        name="outproj_router",
    )(x2d, oa, o_f, o_b, g, hgrn_norm, w_out, norm_ffn, w_router, b_router)


def _moe_tile(T):
    return min(512, T)


def _pack_pairs(x):
    hi = lax.bitcast_convert_type(x[:, :PACK_W], U32)
    lo = lax.bitcast_convert_type(x[:, PACK_W:], U32)
    return (hi & jnp.uint32(0xFFFF0000)) | (lo >> 16)


def _unpack_pairs(u):
    hi = lax.bitcast_convert_type(u & jnp.uint32(0xFFFF0000), F32).astype(BF16)
    lo = lax.bitcast_convert_type(u << 16, F32).astype(BF16)
    return jnp.concatenate([hi, lo], axis=1)


def _stage_rows(tm):
    return pl.cdiv(TOP_K * tm + N_EXPERTS * RUN_ALIGN, STAGE_BLK) * STAGE_BLK


def _run_dmas(tile, run_start_ref, run_len_ref, stage_off_ref, make_copy, act):
    def per_expert(e, carry):
        n = pl.multiple_of(run_len_ref[tile * N_EXPERTS + e], RUN_ALIGN)
        src = pl.multiple_of(stage_off_ref[tile * N_EXPERTS + e], RUN_ALIGN)
        dst = pl.multiple_of(run_start_ref[tile * N_EXPERTS + e], RUN_ALIGN)

        @pl.when(n > 0)
        def _():
            act(make_copy(src, dst, n))

        return carry

    lax.fori_loop(0, N_EXPERTS, per_expert, 0)


def _dispatch_kernel(run_start_ref, run_len_ref, stage_off_ref, pad_start_ref, pad_len_ref, nu_ref,
                     h2_ref, srow_ref, xs_ref, stage, zero_s, sems, zsem):
    i = pl.program_id(0)
    nt = pl.num_programs(0)
    slot = i % 2
    tm = h2_ref.shape[0]
    sr = stage.shape[1]

    def copy_out(s):
        def make(stage_row, hbm_row, size):
            return pltpu.make_async_copy(stage.at[s, pl.ds(stage_row, size)],
                                         xs_ref.at[pl.ds(hbm_row, size)], sems.at[s])
        return make

    tables = (run_start_ref, run_len_ref, stage_off_ref)

    @pl.when(i >= 2)
    def _():
        _run_dmas(i - 2, *tables, copy_out(slot), lambda c: c.wait())

    @pl.when(i == 0)
    def _():
        zero_s[...] = jnp.zeros(zero_s.shape, U32)

        def zero_copy(hbm_row, size):
            return pltpu.make_async_copy(zero_s.at[pl.ds(0, size)], xs_ref.at[pl.ds(hbm_row, size)], zsem)

        def pads(act):
            def per_expert(e, c):
                n = pl.multiple_of(pad_len_ref[e], RUN_ALIGN)

                @pl.when(n > 0)
                def _():
                    act(zero_copy(pl.multiple_of(pad_start_ref[e], RUN_ALIGN), n))

                return c

            lax.fori_loop(0, N_EXPERTS, per_expert, 0)

        def tail(act):
            def per_block(j, c):
                act(zero_copy(pl.multiple_of(j * MOE_ROWS, MOE_ROWS), MOE_ROWS))
                return c

            lax.fori_loop(nu_ref[0], xs_ref.shape[0] // MOE_ROWS, per_block, 0)

        pads(lambda c: c.start())
        tail(lambda c: c.start())
        pads(lambda c: c.wait())
        tail(lambda c: c.wait())

    srow_t = srow_ref[...].T.astype(jnp.int32)
    h2 = h2_ref[...]
    r_iota = lax.broadcasted_iota(jnp.int32, (STAGE_BLK, tm), 0)
    for b in range(sr // STAGE_BLK):
        hit = r_iota == srow_t[0:1, :] - b * STAGE_BLK
        for kslot in range(1, TOP_K):
            hit = hit | (r_iota == srow_t[kslot:kslot + 1, :] - b * STAGE_BLK)
        onehot = jnp.where(hit, 1.0, 0.0).astype(BF16)
        stage[slot, b * STAGE_BLK:(b + 1) * STAGE_BLK, :] = _pack_pairs(
            jnp.dot(onehot, h2, preferred_element_type=F32))
    _run_dmas(i, *tables, copy_out(slot), lambda c: c.start())

    @pl.when(i == nt - 1)
    def _():
        @pl.when(nt >= 2)
        def _():
            _run_dmas(i - 1, *tables, copy_out(1 - slot), lambda c: c.wait())

        _run_dmas(i, *tables, copy_out(slot), lambda c: c.wait())


def _dispatch(h2, srow, run_start, run_len, stage_off, pad_start, pad_len, n_used, n_rows):
    T = h2.shape[0]
    tm = _moe_tile(T)
    row = lambda i, *_: (i, 0)
    return pl.pallas_call(
        _dispatch_kernel,
        grid_spec=pltpu.PrefetchScalarGridSpec(
            num_scalar_prefetch=6,
            grid=(T // tm,),
            in_specs=[
                pl.BlockSpec((tm, D_MODEL), row),
                pl.BlockSpec((tm, LANES), row),
            ],
            out_specs=pl.BlockSpec(memory_space=pl.ANY),
            scratch_shapes=[
                pltpu.VMEM((2, _stage_rows(tm), PACK_W), U32),
                pltpu.VMEM((MOE_ROWS, PACK_W), U32),
                pltpu.SemaphoreType.DMA((2,)),
                pltpu.SemaphoreType.DMA(()),
            ],
        ),
        out_shape=jax.ShapeDtypeStruct((n_rows, PACK_W), U32),
        compiler_params=_params("arbitrary"),
        name="moe_dispatch",
    )(run_start, run_len, stage_off, pad_start, pad_len, n_used, h2, srow)


def _expert_kernel(be_ref, nu_ref, xs_ref, wg_ref, wl_ref, bg_ref, bl_ref, wd_ref, bd_ref, y_ref):
    @pl.when(pl.program_id(0) < nu_ref[0])
    def _():
        x = _unpack_pairs(xs_ref[...])
        glu = jnp.dot(x, wg_ref[...], preferred_element_type=F32) + bg_ref[...]
        lin = jnp.dot(x, wl_ref[...], preferred_element_type=F32) + bl_ref[...]
        glu = jnp.minimum(glu, SWIGLU_LIMIT)
        lin = jnp.clip(lin, -SWIGLU_LIMIT, SWIGLU_LIMIT)
        act = glu * jax.nn.sigmoid(SWIGLU_ALPHA * glu) * (lin + 1.0)
        y = jnp.dot(act.astype(BF16), wd_ref[...], preferred_element_type=F32) + bd_ref[...]
        y_ref[...] = _pack_pairs(y.astype(BF16).astype(F32))

    @pl.when(pl.program_id(0) >= nu_ref[0])
    def _():
        y_ref[...] = jnp.zeros(y_ref.shape, U32)


def _experts(xs, blk_e, n_used, wg, wl, bg, bl, wd, bd):
    n_rows = xs.shape[0]
    nblk = n_rows // MOE_ROWS
    rows = lambda i, be, nu: (jnp.minimum(i, nu[0] - 1), 0)
    per_e = lambda i, be, nu: (be[jnp.minimum(i, nu[0] - 1)], 0, 0)
    return pl.pallas_call(
        _expert_kernel,
        grid_spec=pltpu.PrefetchScalarGridSpec(
            num_scalar_prefetch=2,
            grid=(nblk,),
            in_specs=[
                pl.BlockSpec((MOE_ROWS, PACK_W), rows),
                pl.BlockSpec((None, D_MODEL, D_FF), per_e),
                pl.BlockSpec((None, D_MODEL, D_FF), per_e),
                pl.BlockSpec((None, 1, D_FF), per_e),
                pl.BlockSpec((None, 1, D_FF), per_e),
                pl.BlockSpec((None, D_FF, D_MODEL), per_e),
                pl.BlockSpec((None, 1, D_MODEL), per_e),
            ],
            out_specs=pl.BlockSpec((MOE_ROWS, PACK_W), lambda i, be, nu: (i, 0)),
        ),
        out_shape=jax.ShapeDtypeStruct((n_rows, PACK_W), U32),
        compiler_params=_params("arbitrary"),
        name="moe_experts",
    )(blk_e, n_used, xs, wg, wl, bg, bl, wd, bd)


def _split_kernel(w_ref, perm_ref, g_ref, l_ref):
    grp = 2 * LANES
    for j in range(w_ref.shape[1] // grp):
        w = w_ref[:, j * grp:(j + 1) * grp].astype(BF16)
        r = jnp.dot(w, perm_ref[...], preferred_element_type=F32)
        g_ref[:, j * LANES:(j + 1) * LANES] = r[:, :LANES].astype(BF16)
        l_ref[:, j * LANES:(j + 1) * LANES] = r[:, LANES:].astype(BF16)


def _split_gate_up(w_gate_up):
    E, D, F2 = w_gate_up.shape
    grp = 2 * LANES
    src = jnp.arange(grp, dtype=jnp.int32)[:, None]
    dst = jnp.arange(grp, dtype=jnp.int32)[None, :]
    perm = (dst == (src // 2 + LANES * (src % 2))).astype(BF16)
    half = jax.ShapeDtypeStruct((E, D, F2 // 2), BF16)
    out_spec = pl.BlockSpec((None, D, F2 // 2), lambda e: (e, 0, 0))
    return pl.pallas_call(
        _split_kernel,
        grid=(E,),
        in_specs=[pl.BlockSpec((None, D, F2), lambda e: (e, 0, 0)),
                  pl.BlockSpec((grp, grp), lambda e: (0, 0))],
        out_specs=[out_spec, out_spec],
        out_shape=[half, half],
        compiler_params=_params("parallel"),
        name="split_gate_up",
    )(w_gate_up, perm)


def _combine_kernel(run_start_ref, run_len_ref, stage_off_ref,
                    y_ref, x1_ref, srow_ref, gate_ref, nf_ref, o_ref, stage, sems):
    i = pl.program_id(0)
    nt = pl.num_programs(0)
    slot = i % 2
    tm = x1_ref.shape[0]
    sr = stage.shape[1]

    def copy_in(s):
        def make(stage_row, hbm_row, size):
            return pltpu.make_async_copy(y_ref.at[pl.ds(hbm_row, size)],
                                         stage.at[s, pl.ds(stage_row, size)], sems.at[s])
        return make

    tables = (run_start_ref, run_len_ref, stage_off_ref)

    @pl.when(i == 0)
    def _():
        stage[...] = jnp.zeros(stage.shape, U32)
        _run_dmas(i, *tables, copy_in(0), lambda c: c.start())

    @pl.when(i + 1 < nt)
    def _():
        _run_dmas(i + 1, *tables, copy_in(1 - slot), lambda c: c.start())

    _run_dmas(i, *tables, copy_in(slot), lambda c: c.wait())

    srow = srow_ref[...]
    gates = gate_ref[...]
    lane = lax.broadcasted_iota(jnp.int32, gates.shape, 1)
    pick = lambda a, kslot: jnp.sum(jnp.where(lane == kslot, a, 0.0), axis=1, keepdims=True)
    cols = [pick(srow, kslot).astype(jnp.int32) for kslot in range(TOP_K)]
    g_cols = [pick(gates, kslot) for kslot in range(TOP_K)]
    r_iota = lax.broadcasted_iota(jnp.int32, (tm, STAGE_BLK), 1)
    blocks = []
    for b in range(sr // STAGE_BLK):
        w = jnp.zeros((tm, STAGE_BLK), F32)
        for c, g_k in zip(cols, g_cols):
            w = jnp.where(r_iota == c - b * STAGE_BLK, g_k, w)
        blocks.append(w.astype(BF16))
    weights = jnp.concatenate(blocks, axis=1)
    staged = _unpack_pairs(stage[slot])
    acc = x1_ref[...] + jnp.dot(weights, staged, preferred_element_type=F32)
    ms = jnp.mean(acc * acc, axis=-1, keepdims=True)
    o_ref[...] = acc * lax.rsqrt(ms + EPS) * nf_ref[...]


def _combine(y, x1, srow, gates, run_start, run_len, stage_off, norm_final):
    T = x1.shape[0]
    tm = _moe_tile(T)
    row = lambda i, *_: (i, 0)
    return pl.pallas_call(
        _combine_kernel,
        grid_spec=pltpu.PrefetchScalarGridSpec(
            num_scalar_prefetch=3,
            grid=(T // tm,),
            in_specs=[
                pl.BlockSpec(memory_space=pl.ANY),
                pl.BlockSpec((tm, D_MODEL), row),
                pl.BlockSpec((tm, LANES), row),
                pl.BlockSpec((tm, LANES), row),
                pl.BlockSpec((1, D_MODEL), lambda i, *_: (0, 0)),
            ],
            out_specs=pl.BlockSpec((tm, D_MODEL), row),
            scratch_shapes=[
                pltpu.VMEM((2, _stage_rows(tm), PACK_W), U32),
                pltpu.SemaphoreType.DMA((2,)),
            ],
        ),
        out_shape=jax.ShapeDtypeStruct((T, D_MODEL), F32),
        compiler_params=_params("arbitrary"),
        name="moe_combine",
    )(run_start, run_len, stage_off, y, x1, srow, gates, norm_final)


def _rope_tables(seq):
    half = ROT_DIM // 2
    inv = ROPE_THETA ** (-jnp.arange(0, ROT_DIM, 2, dtype=F32) / ROT_DIM)
    ang = jnp.arange(seq, dtype=F32)[:, None] * inv[None, :]
    cos, sin = jnp.cos(ang), jnp.sin(ang)
    pad = jnp.zeros((seq, HEAD_DIM - ROT_DIM), F32)
    zero = jnp.zeros((seq, half), F32)
    cos_t = jnp.concatenate([cos, cos, pad + 1.0], axis=1)
    sa_t = jnp.concatenate([-sin, zero, pad], axis=1)
    sb_t = jnp.concatenate([zero, sin, pad], axis=1)
    rep = LANES // HEAD_DIM
    return tuple(jnp.tile(t, (1, rep)) for t in (cos_t, sa_t, sb_t))


def _trunk(x, p):
    B, S, _ = x.shape
    T = B * S
    x2d = x.reshape(T, D_MODEL)
    q, k, v, qh, lff, lfb, ih, g = _inproj(x2d, S, p["norm_mix"], p["w_in"],
                                           *(t[:S] for t in p["rope"]), p["lb_fwd"], p["lb_bwd"])
    to3 = lambda a: a.reshape(B, S, a.shape[-1])
    oa = _attention(to3(q), to3(k), to3(v), p["lam"], p["subln_w"])
    o_f, o_b = _hgrn(to3(qh), to3(lff), to3(lfb), to3(ih))
    x1, h2, srow, gates, counts = _outproj(
        x2d, oa.reshape(T, ATT_WIDTH), o_f.reshape(T, HG_WIDTH), o_b.reshape(T, HG_WIDTH), g,
        p["hgrn_norm"], p["w_out"], p["norm_ffn"], p["w_router"], p["b_router"])

    tm = _moe_tile(T)
    nt = T // tm
    n_rows = (pl.cdiv(T * TOP_K + nt * N_EXPERTS * RUN_ALIGN, MOE_ROWS) + N_EXPERTS) * MOE_ROWS
    cnt = counts[:, 0, :N_EXPERTS].astype(jnp.int32)
    run_len = (cnt + RUN_ALIGN - 1) // RUN_ALIGN * RUN_ALIGN
    sizes = jnp.sum(run_len, axis=0)
    padded = (sizes + MOE_ROWS - 1) // MOE_ROWS * MOE_ROWS
    pends = jnp.cumsum(padded)
    pstart = pends - padded
    run_start = pstart[None, :] + jnp.cumsum(run_len, axis=0) - run_len
    stage_off = jnp.cumsum(run_len, axis=1) - run_len
    n_used = (pends[-1:] // MOE_ROWS).astype(jnp.int32)
    blk_row = jnp.arange(n_rows // MOE_ROWS, dtype=jnp.int32) * MOE_ROWS
    blk_e = jnp.minimum(jnp.sum(pends[None, :] <= blk_row[:, None], axis=1),
                        N_EXPERTS - 1).astype(jnp.int32)
    flat = lambda a: a.reshape(-1).astype(jnp.int32)

    xs = _dispatch(h2, srow, flat(run_start), flat(run_len), flat(stage_off),
                   flat(pstart + sizes), flat(padded - sizes), n_used, n_rows)
    y = _experts(xs, blk_e, n_used, p["wg"], p["wl"], p["bg"], p["bl"], p["wd"], p["bd"])
    out = _combine(y, x1, srow, gates, flat(run_start), flat(run_len), flat(stage_off),
                   p["norm_final"])
    return out.reshape(B, S, D_MODEL)


def kernel(x_prompt, x_sample, norm_mix, w_in, lambda_q1, lambda_k1, lambda_q2, lambda_k2, subln_w,
           lb_fwd, lb_bwd, hgrn_norm, w_out, norm_ffn, w_router, b_router, w_gate_up, b_gate_up,
           w_down, b_down, norm_final):
    l = 0
    lam = (jnp.exp(jnp.sum(lambda_q1[l] * lambda_k1[l])) - jnp.exp(jnp.sum(lambda_q2[l] * lambda_k2[l]))
           + LAM_INIT)
    wg, wl = _split_gate_up(w_gate_up[l])
    bgu = b_gate_up[l]
    p = {
        "norm_mix": norm_mix[l][None, :],
        "w_in": w_in[l].astype(BF16),
        "lam": lam.reshape(1).astype(F32),
        "subln_w": subln_w[l][None, :],
        "lb_fwd": jnp.cumsum(jax.nn.softmax(lb_fwd, axis=0), axis=0)[l][None, :],
        "lb_bwd": jnp.cumsum(jax.nn.softmax(lb_bwd, axis=0), axis=0)[l][None, :],
        "hgrn_norm": hgrn_norm[l][None, :],
        "w_out": w_out[l].astype(BF16),
        "norm_ffn": norm_ffn[l][None, :],
        "w_router": jnp.pad(w_router[l], ((0, 0), (0, LANES - N_EXPERTS))),
        "b_router": jnp.pad(b_router[l], (0, LANES - N_EXPERTS), constant_values=-jnp.inf)[None, :],
        "wg": wg,
        "wl": wl,
        "bg": bgu[:, None, 0::2],
        "bl": bgu[:, None, 1::2],
        "wd": w_down[l].astype(BF16),
        "bd": b_down[l][:, None, :],
        "norm_final": norm_final[None, :],
        "rope": _rope_tables(max(x_prompt.shape[1], x_sample.shape[1])),
    }
    return (_trunk(x_prompt, p), _trunk(x_sample, p))
```

```python
import math

import jax
import jax.numpy as jnp
from jax import lax
from jax.experimental import pallas as pl
from jax.experimental.pallas import tpu as pltpu

D_MODEL = 1024
HEAD_DIM = 64
ATT_HEADS = 4
ATT_V = 2 * HEAD_DIM
ATT_WIDTH = ATT_HEADS * ATT_V
ROT_DIM = HEAD_DIM // 4
ROPE_THETA = 500000.0
HG_HEADS = 4
HG_K = 128
HG_WIDTH = HG_HEADS * HG_K
IN_COLS = 4096
GROUP_COLS = 512
N_EXPERTS = 32
TOP_K = 4
D_FF = 1024
SWIGLU_LIMIT = 7.0
SWIGLU_ALPHA = 1.702
EPS = 1e-5
LAM_INIT = 0.8 - 0.6 * math.exp(-0.3 * 0)

LANES = 128
SUBLANES = 8
RUN_ALIGN = SUBLANES
STAGE_BLK = 256
assert STAGE_BLK <= 256
HG_CHUNK = 128
HG_BLOCK = 256
ATT_TK = 256
ATT_UNROLL = 16
LOG2_E = math.log2(math.e)
MOE_ROWS = 512
VMEM_LIMIT = 56 * 1024 * 1024

F32 = jnp.float32
BF16 = jnp.bfloat16
U32 = jnp.uint32
PACK_W = D_MODEL // 2
NT_DIMS = (((1,), (1,)), ((), ()))
TN_DIMS = (((0,), (0,)), ((), ()))


def _params(*sem):
    return pltpu.CompilerParams(dimension_semantics=sem, vmem_limit_bytes=VMEM_LIMIT)


def _inproj_kernel(x_ref, nw_ref, w_ref, cos_ref, sa_ref, sb_ref, lbf_ref, lbb_ref,
                   q_ref, k_ref, v_ref, qh_ref, lff_ref, lfb_ref, ih_ref, g_ref):
    x = x_ref[...]
    ms = jnp.mean(x * x, axis=-1, keepdims=True)
    h = (x * lax.rsqrt(ms + EPS) * nw_ref[...]).astype(BF16)

    def proj(c):
        return jnp.dot(h, w_ref[:, c * GROUP_COLS:(c + 1) * GROUP_COLS],
                       preferred_element_type=F32)

    cos, sa, sb = cos_ref[...], sa_ref[...], sb_ref[...]

    def rope(p):
        outs = []
        for j in range(GROUP_COLS // LANES):
            xx = p[:, j * LANES:(j + 1) * LANES]
            outs.append(xx * cos + pltpu.roll(xx, LANES - ROT_DIM // 2, 1) * sa
                        + pltpu.roll(xx, ROT_DIM // 2, 1) * sb)
        return jnp.concatenate(outs, axis=1)

    def log_forget(p, lb):
        return jnp.log(lb + (1.0 - lb) * jax.nn.sigmoid(p))

    q_ref[...] = (rope(proj(0)) * (HEAD_DIM ** -0.5 * LOG2_E)).astype(BF16)
    k_ref[...] = rope(proj(1)).astype(BF16)
    va = proj(2).astype(BF16)
    ones = jnp.ones((va.shape[0], ATT_V), BF16)
    for hh in range(ATT_HEADS):
        v_ref[:, 2 * hh * ATT_V:(2 * hh + 1) * ATT_V] = va[:, hh * ATT_V:(hh + 1) * ATT_V]
        v_ref[:, (2 * hh + 1) * ATT_V:(2 * hh + 2) * ATT_V] = ones
    qh_ref[...] = proj(3).astype(BF16)
    lff_ref[...] = log_forget(proj(4), lbf_ref[...])
    lfb_ref[...] = log_forget(proj(5), lbb_ref[...])
    ih_ref[...] = proj(6).astype(BF16)
    gh = proj(7)
    g_ref[...] = (gh * jax.nn.sigmoid(gh)).astype(BF16)


def _inproj(x2d, seq, norm_w, w_in, cos_t, sa_t, sb_t, lbf, lbb):
    T = x2d.shape[0]
    tm = min(512, seq)
    nseq = seq // tm
    row = lambda i: (i, 0)
    fixed = lambda i: (0, 0)
    tab = lambda i: (i % nseq, 0)
    out_dt = [BF16, BF16, BF16, BF16, F32, F32, BF16, BF16]
    out_w = [GROUP_COLS, GROUP_COLS, 2 * ATT_WIDTH] + [GROUP_COLS] * 5
    return pl.pallas_call(
        _inproj_kernel,
        grid=(T // tm,),
        in_specs=[
            pl.BlockSpec((tm, D_MODEL), row),
            pl.BlockSpec((1, D_MODEL), fixed),
            pl.BlockSpec((D_MODEL, IN_COLS), fixed),
            pl.BlockSpec((tm, LANES), tab),
            pl.BlockSpec((tm, LANES), tab),
            pl.BlockSpec((tm, LANES), tab),
            pl.BlockSpec((1, GROUP_COLS), fixed),
            pl.BlockSpec((1, GROUP_COLS), fixed),
        ],
        out_specs=[pl.BlockSpec((tm, w), row) for w in out_w],
        out_shape=[jax.ShapeDtypeStruct((T, w), dt) for w, dt in zip(out_w, out_dt)],
        compiler_params=_params("parallel"),
        name="inproj",
    )(x2d, norm_w, w_in, cos_t, sa_t, sb_t, lbf, lbb)


def _attn_kernel(lam_ref, q_ref, k_ref, v_ref, w_ref, o_ref, m_s, acc_s):
    n_chunks = k_ref.shape[0] // ATT_TK
    q = q_ref[...]
    lane = lax.broadcasted_iota(jnp.int32, q.shape, 1)
    zero = jnp.zeros_like(q)
    qz = (jnp.where(lane < HEAD_DIM, q, zero), jnp.where(lane >= HEAD_DIM, q, zero))
    m_s[...] = jnp.full(m_s.shape, -jnp.inf, F32)
    acc_s[...] = jnp.zeros(acc_s.shape, F32)

    def chunk(j):
        start = pl.multiple_of(j * ATT_TK, ATT_TK)
        kc = k_ref[pl.ds(start, ATT_TK), :]
        vc = v_ref[pl.ds(start, ATT_TK), :]
        for c in range(2):
            s = lax.dot_general(qz[c], kc, NT_DIMS, preferred_element_type=F32)
            m_prev = m_s[c]
            m_next = jnp.maximum(m_prev, jnp.max(s, axis=1, keepdims=True))
            alpha = jnp.exp2(m_prev - m_next)
            p = jnp.exp2(s - jnp.tile(m_next, (1, ATT_TK // LANES))).astype(BF16)
            acc_s[c] = acc_s[c] * jnp.tile(alpha, (1, 2)) + jnp.dot(p, vc, preferred_element_type=F32)
            m_s[c] = m_next

    unroll = math.gcd(n_chunks, ATT_UNROLL)

    def body(jj, carry):
        for u in range(unroll):
            chunk(jj * unroll + u)
        return carry

    lax.fori_loop(0, n_chunks // unroll, body, 0)

    lam = lam_ref[0]
    a1, a2 = acc_s[0], acc_s[1]
    o = a1[:, :ATT_V] / a1[:, ATT_V:] - lam * (a2[:, :ATT_V] / a2[:, ATT_V:])
    ms = jnp.mean(o * o, axis=-1, keepdims=True)
    o = o * lax.rsqrt(ms + EPS) * w_ref[...] * (1.0 - LAM_INIT)
    o_ref[...] = o.astype(o_ref.dtype)


def _attention(q, k, v_ext, lam, subln_w):
    B, S, _ = q.shape
    tq = min(1024, S)
    return pl.pallas_call(
        _attn_kernel,
        grid=(B, ATT_HEADS, S // tq),
        in_specs=[
            pl.BlockSpec(memory_space=pltpu.SMEM),
            pl.BlockSpec((None, tq, ATT_V), lambda b, h, i: (b, i, h)),
            pl.BlockSpec((None, S, ATT_V), lambda b, h, i: (b, 0, h)),
            pl.BlockSpec((None, S, 2 * ATT_V), lambda b, h, i: (b, 0, h)),
            pl.BlockSpec((1, ATT_V), lambda b, h, i: (0, 0)),
        ],
        out_specs=pl.BlockSpec((None, tq, ATT_V), lambda b, h, i: (b, i, h)),
        out_shape=jax.ShapeDtypeStruct((B, S, ATT_WIDTH), BF16),
        scratch_shapes=[
            pltpu.VMEM((2, tq, LANES), F32),
            pltpu.VMEM((2, tq, 2 * ATT_V), F32),
        ],
        compiler_params=_params("parallel", "parallel", "arbitrary"),
        name="diff_attention",
    )(lam, q, k, v_ext, subln_w)


def _hgrn_kernel(qf_ref, lff_ref, vf_ref, qb_ref, lfb_ref, vb_ref, lvlf_ref, lvlb_ref,
                 of_ref, ob_ref, stf_ref, stb_ref):
    @pl.when(pl.program_id(1) == 0)
    def _():
        stf_ref[...] = jnp.zeros(stf_ref.shape, F32)
        stb_ref[...] = jnp.zeros(stb_ref.shape, F32)

    n_sub = qf_ref.shape[0] // HG_CHUNK
    for sub in range(n_sub):
        f_rows = pl.ds(sub * HG_CHUNK, HG_CHUNK)
        b_rows = pl.ds((n_sub - 1 - sub) * HG_CHUNK, HG_CHUNK)
        _hgrn_chunk(qf_ref.at[f_rows], lff_ref.at[f_rows], vf_ref.at[f_rows], lvlf_ref,
                    of_ref.at[f_rows], stf_ref, reverse=False)
        _hgrn_chunk(qb_ref.at[b_rows], lfb_ref.at[b_rows], vb_ref.at[b_rows], lvlb_ref,
                    ob_ref.at[b_rows], stb_ref, reverse=True)


def _hgrn_chunk(q_ref, lf_ref, v_ref, lvl_ref, o_ref, st_ref, *, reverse):
    C = q_ref.shape[0]

    lf = lf_ref[...]
    q = q_ref[...].astype(F32)
    v = v_ref[...]
    kk = 1.0 - jnp.exp(lf)
    row = lax.broadcasted_iota(jnp.int32, lf.shape, 0)
    hi_side, lo_side = (kk, q) if reverse else (q, kk)

    def halves(x, m):
        lo = [x[b:b + m] for b in range(0, C, 2 * m)]
        up = [x[b + m:b + 2 * m] for b in range(0, C, 2 * m)]
        return jnp.concatenate(lo, axis=0), jnp.concatenate(up, axis=0)

    def merge(lo, up, m):
        parts = []
        for b in range(0, C // 2, m):
            parts += [lo[b:b + m], up[b:b + m]]
        return jnp.concatenate(parts, axis=0)

    pref, tot = lf, lf
    xs = []
    m = 1
    while m < SUBLANES:
        upper = (row & m) != 0
        if reverse:
            g = jnp.where(upper, pref - lf, tot - pref + lf)
        else:
            g = jnp.where(upper, pref, tot - pref)
        xs.append((jnp.where(upper, hi_side, lo_side) * jnp.exp(g)).astype(BF16))
        grouped = tot.reshape(C // SUBLANES, SUBLANES, tot.shape[1])
        below = pltpu.roll(grouped, m, 1).reshape(tot.shape)
        above = pltpu.roll(grouped, SUBLANES - m, 1).reshape(tot.shape)
        pref = jnp.where(upper, pref + below, pref)
        tot = tot + jnp.where(upper, below, above)
        m *= 2
    while m < C:
        p_lo, p_up = halves(pref, m)
        t_lo, t_up = halves(tot, m)
        hi_up = halves(hi_side, m)[1]
        lo_lo = halves(lo_side, m)[0]
        if reverse:
            lf_lo, lf_up = halves(lf, m)
            g_up, g_lo = p_up - lf_up, t_lo - p_lo + lf_lo
        else:
            g_up, g_lo = p_up, t_lo - p_lo
        xs.append(merge(lo_lo * jnp.exp(g_lo), hi_up * jnp.exp(g_up), m).astype(BF16))
        t_new = t_lo + t_up
        pref = merge(p_lo, p_up + t_lo, m)
        tot = merge(t_new, t_new, m)
        m *= 2

    if reverse:
        gq, gk = tot - pref + lf, pref - lf
    else:
        gq, gk = pref, tot - pref
    xq = (q * jnp.exp(gq)).astype(BF16)
    xk = (kk * jnp.exp(gk)).astype(BF16)
    chunk_decay = jnp.exp(tot[0:1, :])
    qk = q * kk
    lvl = lvl_ref[...]
    at_level = [lvl == li + 1 for li in range(len(xs))]

    for h in range(HG_HEADS):
        sl = slice(h * HG_K, (h + 1) * HG_K)
        a = jnp.zeros((C, C), F32)
        for li, x in enumerate(xs):
            xh = x[:, sl]
            am = lax.dot_general(xh, xh, NT_DIMS, preferred_element_type=F32)
            a = jnp.where(at_level[li], am, a)
        vh = v[:, sl]
        st = st_ref[h]
        o = jnp.dot(a.astype(BF16), vh, preferred_element_type=F32)
        o += lax.dot_general(xq[:, sl], st.astype(BF16), NT_DIMS, preferred_element_type=F32)
        o += jnp.sum(qk[:, sl], axis=1, keepdims=True) * vh.astype(F32)
        o_ref[:, sl] = o
        st_ref[h] = st * chunk_decay[:, sl] + lax.dot_general(
            vh, xk[:, sl], TN_DIMS, preferred_element_type=F32)


def _level_table(reverse):
    t = jnp.arange(HG_CHUNK, dtype=jnp.int32)[:, None]
    s = jnp.arange(HG_CHUNK, dtype=jnp.int32)[None, :]
    x = t ^ s
    lvl = jnp.zeros_like(x)
    for b in range(HG_CHUNK.bit_length() - 1):
        lvl = jnp.where(x >= (1 << b), b + 1, lvl)
    active = (s > t) if reverse else (s < t)
    return jnp.where(active, lvl, 0)


def _hgrn(qh, lf_f, lf_b, vh):
    B, S, _ = qh.shape
    rows = math.gcd(S, HG_BLOCK)
    nc = S // rows
    fwd = pl.BlockSpec((None, rows, HG_WIDTH), lambda b, c: (b, c, 0))
    bwd = pl.BlockSpec((None, rows, HG_WIDTH), lambda b, c: (b, nc - 1 - c, 0))
    lvl = pl.BlockSpec((HG_CHUNK, HG_CHUNK), lambda b, c: (0, 0))
    out = jax.ShapeDtypeStruct((B, S, HG_WIDTH), F32)
    state = pltpu.VMEM((HG_HEADS, HG_K, HG_K), F32)
    return pl.pallas_call(
        _hgrn_kernel,
        grid=(B, nc),
        in_specs=[fwd, fwd, fwd, bwd, bwd, bwd, lvl, lvl],
        out_specs=[fwd, bwd],
        out_shape=[out, out],
        scratch_shapes=[state, state],
        compiler_params=_params("parallel", "arbitrary"),
        name="hgrn",
    )(qh, lf_f, vh, qh, lf_b, vh, _level_table(False), _level_table(True))


def _outproj_kernel(x_ref, oa_ref, of_ref, ob_ref, g_ref, hn_ref, wo_ref, fn_ref, wr_ref, br_ref,
                    x1_ref, h2_ref, srow_ref, gate_ref, cnt_ref):
    tm = x_ref.shape[0]

    oh = of_ref[...] + ob_ref[...]
    ms = jnp.mean(oh * oh, axis=-1, keepdims=True)
    ohn = oh * lax.rsqrt(ms + EPS) * hn_ref[...] * g_ref[...].astype(F32)
    mix = jnp.dot(oa_ref[...], wo_ref[:ATT_WIDTH, :], preferred_element_type=F32)
    mix += jnp.dot(ohn.astype(BF16), wo_ref[ATT_WIDTH:, :], preferred_element_type=F32)
    x1 = x_ref[...] + mix
    x1_ref[...] = x1
    ms = jnp.mean(x1 * x1, axis=-1, keepdims=True)
    h2 = x1 * lax.rsqrt(ms + EPS) * fn_ref[...]
    h2_ref[...] = h2.astype(BF16)

    wr = wr_ref[...]
    w_hi = wr.astype(BF16)
    w_lo = (wr - w_hi.astype(F32)).astype(BF16)
    h_hi = h2.astype(BF16)
    h_lo = (h2 - h_hi.astype(F32)).astype(BF16)
    logits = (jnp.dot(h_hi, w_hi, preferred_element_type=F32)
              + jnp.dot(h_hi, w_lo, preferred_element_type=F32)
              + jnp.dot(h_lo, w_hi, preferred_element_type=F32)) + br_ref[...]
    lane = lax.broadcasted_iota(jnp.int32, logits.shape, 1)
    lane_f = lane.astype(F32)
    vals, idxs = [], []
    for _ in range(TOP_K):
        mx = jnp.max(logits, axis=1, keepdims=True)
        ix = jnp.min(jnp.where(logits == mx, lane_f, float(LANES)), axis=1, keepdims=True)
        vals.append(mx)
        idxs.append(ix)
        logits = jnp.where(lane_f == ix, -jnp.inf, logits)
    es = [jnp.exp(vv - vals[0]) for vv in vals]
    den = es[0] + es[1] + es[2] + es[3]

    onehots = [(lane_f == ix).astype(F32) for ix in idxs]
    multi = onehots[0] + onehots[1] + onehots[2] + onehots[3]
    r_i = lax.broadcasted_iota(jnp.int32, (tm, tm), 0)
    c_i = lax.broadcasted_iota(jnp.int32, (tm, tm), 1)
    tri = jnp.where(c_i < r_i, 1.0, 0.0).astype(BF16)
    before = jnp.dot(tri, multi.astype(BF16), preferred_element_type=F32)

    cnt = jnp.broadcast_to(jnp.sum(multi, axis=0, keepdims=True), cnt_ref.shape)
    run_len = jnp.ceil(cnt * (1.0 / RUN_ALIGN)) * RUN_ALIGN
    e_i = lax.broadcasted_iota(jnp.int32, (LANES, LANES), 0)
    e_j = lax.broadcasted_iota(jnp.int32, (LANES, LANES), 1)
    lower = jnp.where(e_i < e_j, 1.0, 0.0)
    stage_off = jnp.dot(run_len, lower, preferred_element_type=F32,
                        precision=lax.Precision.HIGHEST)[0:1, :]
    srow = jnp.full(logits.shape, -1.0, F32)
    gates = jnp.zeros(logits.shape, F32)
    for kslot in range(TOP_K):
        here = lane == kslot
        gates = jnp.where(here, es[kslot] / den, gates)
        row_k = jnp.sum(onehots[kslot] * (before + stage_off), axis=1, keepdims=True)
        srow = jnp.where(here, row_k, srow)
    srow_ref[...] = srow
    gate_ref[...] = gates
    cnt_ref[...] = cnt


def _outproj(x2d, oa, o_f, o_b, g, hgrn_norm, w_out, norm_ffn, w_router, b_router):
    T = x2d.shape[0]
    tm = _moe_tile(T)
    row = lambda i: (i, 0)
    fixed = lambda i: (0, 0)
    return pl.pallas_call(
        _outproj_kernel,
        grid=(T // tm,),
        in_specs=[
            pl.BlockSpec((tm, D_MODEL), row),
            pl.BlockSpec((tm, ATT_WIDTH), row),
            pl.BlockSpec((tm, HG_WIDTH), row),
            pl.BlockSpec((tm, HG_WIDTH), row),
            pl.BlockSpec((tm, HG_WIDTH), row),
            pl.BlockSpec((1, HG_WIDTH), fixed),
            pl.BlockSpec((D_MODEL, D_MODEL), fixed),
            pl.BlockSpec((1, D_MODEL), fixed),
            pl.BlockSpec((D_MODEL, LANES), fixed),
            pl.BlockSpec((1, LANES), fixed),
        ],
        out_specs=[
            pl.BlockSpec((tm, D_MODEL), row),
            pl.BlockSpec((tm, D_MODEL), row),
            pl.BlockSpec((tm, LANES), row),
            pl.BlockSpec((tm, LANES), row),
            pl.BlockSpec((None, SUBLANES, LANES), lambda i: (i, 0, 0)),
        ],
        out_shape=[
            jax.ShapeDtypeStruct((T, D_MODEL), F32),
            jax.ShapeDtypeStruct((T, D_MODEL), BF16),
            jax.ShapeDtypeStruct((T, LANES), F32),
            jax.ShapeDtypeStruct((T, LANES), F32),
            jax.ShapeDtypeStruct((T // tm, SUBLANES, LANES), F32),
        ],
        compiler_params=_params("parallel"),
        name="outproj_router",
    )(x2d, oa, o_f, o_b, g, hgrn_norm, w_out, norm_ffn, w_router, b_router)


def _moe_tile(T):
    return min(512, T)


def _pack_pairs(x):
    hi = lax.bitcast_convert_type(x[:, :PACK_W], U32)
    lo = lax.bitcast_convert_type(x[:, PACK_W:], U32)
    return (hi & jnp.uint32(0xFFFF0000)) | (lo >> 16)


def _unpack_pairs(u):
    hi = lax.bitcast_convert_type(u & jnp.uint32(0xFFFF0000), F32).astype(BF16)
    lo = lax.bitcast_convert_type(u << 16, F32).astype(BF16)
    return jnp.concatenate([hi, lo], axis=1)


def _stage_rows(tm):
    return pl.cdiv(TOP_K * tm + N_EXPERTS * RUN_ALIGN, STAGE_BLK) * STAGE_BLK


def _run_dmas(tile, run_start_ref, run_len_ref, stage_off_ref, make_copy, act):
    def per_expert(e, carry):
        n = pl.multiple_of(run_len_ref[tile * N_EXPERTS + e], RUN_ALIGN)
        src = pl.multiple_of(stage_off_ref[tile * N_EXPERTS + e], RUN_ALIGN)
        dst = pl.multiple_of(run_start_ref[tile * N_EXPERTS + e], RUN_ALIGN)

        @pl.when(n > 0)
        def _():
            act(make_copy(src, dst, n))

        return carry

    lax.fori_loop(0, N_EXPERTS, per_expert, 0)


def _dispatch_kernel(run_start_ref, run_len_ref, stage_off_ref, pad_start_ref, pad_len_ref, nu_ref,
                     h2_ref, srow_ref, xs_ref, stage, zero_s, sems, zsem):
    i = pl.program_id(0)
    nt = pl.num_programs(0)
    slot = i % 2
    tm = h2_ref.shape[0]
    sr = stage.shape[1]

    def copy_out(s):
        def make(stage_row, hbm_row, size):
            return pltpu.make_async_copy(stage.at[s, pl.ds(stage_row, size)],
                                         xs_ref.at[pl.ds(hbm_row, size)], sems.at[s])
        return make

    tables = (run_start_ref, run_len_ref, stage_off_ref)

    @pl.when(i >= 2)
    def _():
        _run_dmas(i - 2, *tables, copy_out(slot), lambda c: c.wait())

    @pl.when(i == 0)
    def _():
        zero_s[...] = jnp.zeros(zero_s.shape, U32)

        def zero_copy(hbm_row, size):
            return pltpu.make_async_copy(zero_s.at[pl.ds(0, size)], xs_ref.at[pl.ds(hbm_row, size)], zsem)

        def pads(act):
            def per_expert(e, c):
                n = pl.multiple_of(pad_len_ref[e], RUN_ALIGN)

                @pl.when(n > 0)
                def _():
                    act(zero_copy(pl.multiple_of(pad_start_ref[e], RUN_ALIGN), n))

                return c

            lax.fori_loop(0, N_EXPERTS, per_expert, 0)

        def tail(act):
            def per_block(j, c):
                act(zero_copy(pl.multiple_of(j * MOE_ROWS, MOE_ROWS), MOE_ROWS))
                return c

            lax.fori_loop(nu_ref[0], xs_ref.shape[0] // MOE_ROWS, per_block, 0)

        pads(lambda c: c.start())
        tail(lambda c: c.start())
        pads(lambda c: c.wait())
        tail(lambda c: c.wait())

    srow_t = srow_ref[...].T
    h2 = h2_ref[...]
    r_iota = lax.broadcasted_iota(jnp.int32, (STAGE_BLK, tm), 0).astype(F32).astype(BF16)
    one, zero = jnp.ones((STAGE_BLK, tm), BF16), jnp.zeros((STAGE_BLK, tm), BF16)
    for b in range(sr // STAGE_BLK):
        rel = [(srow_t[kslot:kslot + 1, :] - b * STAGE_BLK).astype(BF16) for kslot in range(TOP_K)]
        hit = r_iota == rel[0]
        for kslot in range(1, TOP_K):
            hit = hit | (r_iota == rel[kslot])
        onehot = jnp.where(hit, one, zero)
        stage[slot, b * STAGE_BLK:(b + 1) * STAGE_BLK, :] = _pack_pairs(
            jnp.dot(onehot, h2, preferred_element_type=F32))
    _run_dmas(i, *tables, copy_out(slot), lambda c: c.start())

    @pl.when(i == nt - 1)
    def _():
        @pl.when(nt >= 2)
        def _():
            _run_dmas(i - 1, *tables, copy_out(1 - slot), lambda c: c.wait())

        _run_dmas(i, *tables, copy_out(slot), lambda c: c.wait())


def _dispatch(h2, srow, run_start, run_len, stage_off, pad_start, pad_len, n_used, n_rows):
    T = h2.shape[0]
    tm = _moe_tile(T)
    row = lambda i, *_: (i, 0)
    return pl.pallas_call(
        _dispatch_kernel,
        grid_spec=pltpu.PrefetchScalarGridSpec(
            num_scalar_prefetch=6,
            grid=(T // tm,),
            in_specs=[
                pl.BlockSpec((tm, D_MODEL), row),
                pl.BlockSpec((tm, LANES), row),
            ],
            out_specs=pl.BlockSpec(memory_space=pl.ANY),
            scratch_shapes=[
                pltpu.VMEM((2, _stage_rows(tm), PACK_W), U32),
                pltpu.VMEM((MOE_ROWS, PACK_W), U32),
                pltpu.SemaphoreType.DMA((2,)),
                pltpu.SemaphoreType.DMA(()),
            ],
        ),
        out_shape=jax.ShapeDtypeStruct((n_rows, PACK_W), U32),
        compiler_params=_params("arbitrary"),
        name="moe_dispatch",
    )(run_start, run_len, stage_off, pad_start, pad_len, n_used, h2, srow)


def _expert_kernel(be_ref, nu_ref, xs_ref, wg_ref, wl_ref, bg_ref, bl_ref, wd_ref, bd_ref, y_ref):
    @pl.when(pl.program_id(0) < nu_ref[0])
    def _():
        x = _unpack_pairs(xs_ref[...])
        glu = jnp.dot(x, wg_ref[...], preferred_element_type=F32) + bg_ref[...]
        lin = jnp.dot(x, wl_ref[...], preferred_element_type=F32) + bl_ref[...]
        glu = jnp.minimum(glu, SWIGLU_LIMIT)
        lin = jnp.clip(lin, -SWIGLU_LIMIT, SWIGLU_LIMIT)
        act = glu * jax.nn.sigmoid(SWIGLU_ALPHA * glu) * (lin + 1.0)
        y = jnp.dot(act.astype(BF16), wd_ref[...], preferred_element_type=F32) + bd_ref[...]
        y_ref[...] = _pack_pairs(y.astype(BF16).astype(F32))

    @pl.when(pl.program_id(0) >= nu_ref[0])
    def _():
        y_ref[...] = jnp.zeros(y_ref.shape, U32)


def _experts(xs, blk_e, n_used, wg, wl, bg, bl, wd, bd):
    n_rows = xs.shape[0]
    nblk = n_rows // MOE_ROWS
    rows = lambda i, be, nu: (jnp.minimum(i, nu[0] - 1), 0)
    per_e = lambda i, be, nu: (be[jnp.minimum(i, nu[0] - 1)], 0, 0)
    return pl.pallas_call(
        _expert_kernel,
        grid_spec=pltpu.PrefetchScalarGridSpec(
            num_scalar_prefetch=2,
            grid=(nblk,),
            in_specs=[
                pl.BlockSpec((MOE_ROWS, PACK_W), rows),
                pl.BlockSpec((None, D_MODEL, D_FF), per_e),
                pl.BlockSpec((None, D_MODEL, D_FF), per_e),
                pl.BlockSpec((None, 1, D_FF), per_e),
                pl.BlockSpec((None, 1, D_FF), per_e),
                pl.BlockSpec((None, D_FF, D_MODEL), per_e),
                pl.BlockSpec((None, 1, D_MODEL), per_e),
            ],
            out_specs=pl.BlockSpec((MOE_ROWS, PACK_W), lambda i, be, nu: (i, 0)),
        ),
        out_shape=jax.ShapeDtypeStruct((n_rows, PACK_W), U32),
        compiler_params=_params("arbitrary"),
        name="moe_experts",
    )(blk_e, n_used, xs, wg, wl, bg, bl, wd, bd)


def _split_kernel(w_ref, perm_ref, g_ref, l_ref):
    grp = 2 * LANES
    for j in range(w_ref.shape[1] // grp):
        w = w_ref[:, j * grp:(j + 1) * grp].astype(BF16)
        r = jnp.dot(w, perm_ref[...], preferred_element_type=F32)
        g_ref[:, j * LANES:(j + 1) * LANES] = r[:, :LANES].astype(BF16)
        l_ref[:, j * LANES:(j + 1) * LANES] = r[:, LANES:].astype(BF16)


def _split_gate_up(w_gate_up):
    E, D, F2 = w_gate_up.shape
    grp = 2 * LANES
    src = jnp.arange(grp, dtype=jnp.int32)[:, None]
    dst = jnp.arange(grp, dtype=jnp.int32)[None, :]
    perm = (dst == (src // 2 + LANES * (src % 2))).astype(BF16)
    half = jax.ShapeDtypeStruct((E, D, F2 // 2), BF16)
    out_spec = pl.BlockSpec((None, D, F2 // 2), lambda e: (e, 0, 0))
    return pl.pallas_call(
        _split_kernel,
        grid=(E,),
        in_specs=[pl.BlockSpec((None, D, F2), lambda e: (e, 0, 0)),
                  pl.BlockSpec((grp, grp), lambda e: (0, 0))],
        out_specs=[out_spec, out_spec],
        out_shape=[half, half],
        compiler_params=_params("parallel"),
        name="split_gate_up",
    )(w_gate_up, perm)


def _combine_kernel(run_start_ref, run_len_ref, stage_off_ref,
                    y_ref, x1_ref, srow_ref, gate_ref, nf_ref, o_ref, stage, sems):
    i = pl.program_id(0)
    nt = pl.num_programs(0)
    slot = i % 2
    tm = x1_ref.shape[0]
    sr = stage.shape[1]

    def copy_in(s):
        def make(stage_row, hbm_row, size):
            return pltpu.make_async_copy(y_ref.at[pl.ds(hbm_row, size)],
                                         stage.at[s, pl.ds(stage_row, size)], sems.at[s])
        return make

    tables = (run_start_ref, run_len_ref, stage_off_ref)

    @pl.when(i == 0)
    def _():
        stage[...] = jnp.zeros(stage.shape, U32)
        _run_dmas(i, *tables, copy_in(0), lambda c: c.start())

    @pl.when(i + 1 < nt)
    def _():
        _run_dmas(i + 1, *tables, copy_in(1 - slot), lambda c: c.start())

    _run_dmas(i, *tables, copy_in(slot), lambda c: c.wait())

    srow = srow_ref[...]
    gates = gate_ref[...]
    lane = lax.broadcasted_iota(jnp.int32, gates.shape, 1)
    pick = lambda a, kslot: jnp.sum(jnp.where(lane == kslot, a, 0.0), axis=1, keepdims=True)
    cols = [pick(srow, kslot) for kslot in range(TOP_K)]
    g_cols = [jnp.broadcast_to(pick(gates, kslot).astype(BF16), (tm, STAGE_BLK))
              for kslot in range(TOP_K)]
    r_iota = lax.broadcasted_iota(jnp.int32, (tm, STAGE_BLK), 1).astype(F32).astype(BF16)
    blocks = []
    for b in range(sr // STAGE_BLK):
        w = jnp.zeros((tm, STAGE_BLK), BF16)
        for c, g_k in zip(cols, g_cols):
            w = jnp.where(r_iota == (c - b * STAGE_BLK).astype(BF16), g_k, w)
        blocks.append(w)
    weights = jnp.concatenate(blocks, axis=1)
    staged = _unpack_pairs(stage[slot])
    acc = x1_ref[...] + jnp.dot(weights, staged, preferred_element_type=F32)
    ms = jnp.mean(acc * acc, axis=-1, keepdims=True)
    o_ref[...] = acc * lax.rsqrt(ms + EPS) * nf_ref[...]


def _combine(y, x1, srow, gates, run_start, run_len, stage_off, norm_final):
    T = x1.shape[0]
    tm = _moe_tile(T)
    row = lambda i, *_: (i, 0)
    return pl.pallas_call(
        _combine_kernel,
        grid_spec=pltpu.PrefetchScalarGridSpec(
            num_scalar_prefetch=3,
            grid=(T // tm,),
            in_specs=[
                pl.BlockSpec(memory_space=pl.ANY),
                pl.BlockSpec((tm, D_MODEL), row),
                pl.BlockSpec((tm, LANES), row),
                pl.BlockSpec((tm, LANES), row),
                pl.BlockSpec((1, D_MODEL), lambda i, *_: (0, 0)),
            ],
            out_specs=pl.BlockSpec((tm, D_MODEL), row),
            scratch_shapes=[
                pltpu.VMEM((2, _stage_rows(tm), PACK_W), U32),
                pltpu.SemaphoreType.DMA((2,)),
            ],
        ),
        out_shape=jax.ShapeDtypeStruct((T, D_MODEL), F32),
        compiler_params=_params("arbitrary"),
        name="moe_combine",
    )(run_start, run_len, stage_off, y, x1, srow, gates, norm_final)


def _rope_tables(seq):
    half = ROT_DIM // 2
    inv = ROPE_THETA ** (-jnp.arange(0, ROT_DIM, 2, dtype=F32) / ROT_DIM)
    ang = jnp.arange(seq, dtype=F32)[:, None] * inv[None, :]
    cos, sin = jnp.cos(ang), jnp.sin(ang)
    pad = jnp.zeros((seq, HEAD_DIM - ROT_DIM), F32)
    zero = jnp.zeros((seq, half), F32)
    cos_t = jnp.concatenate([cos, cos, pad + 1.0], axis=1)
    sa_t = jnp.concatenate([-sin, zero, pad], axis=1)
    sb_t = jnp.concatenate([zero, sin, pad], axis=1)
    rep = LANES // HEAD_DIM
    return tuple(jnp.tile(t, (1, rep)) for t in (cos_t, sa_t, sb_t))


def _trunk(x, p):
    B, S, _ = x.shape
    T = B * S
    x2d = x.reshape(T, D_MODEL)
    q, k, v, qh, lff, lfb, ih, g = _inproj(x2d, S, p["norm_mix"], p["w_in"],
                                           *(t[:S] for t in p["rope"]), p["lb_fwd"], p["lb_bwd"])
    to3 = lambda a: a.reshape(B, S, a.shape[-1])
    oa = _attention(to3(q), to3(k), to3(v), p["lam"], p["subln_w"])
    o_f, o_b = _hgrn(to3(qh), to3(lff), to3(lfb), to3(ih))
    x1, h2, srow, gates, counts = _outproj(
        x2d, oa.reshape(T, ATT_WIDTH), o_f.reshape(T, HG_WIDTH), o_b.reshape(T, HG_WIDTH), g,
        p["hgrn_norm"], p["w_out"], p["norm_ffn"], p["w_router"], p["b_router"])

    tm = _moe_tile(T)
    nt = T // tm
    n_rows = (pl.cdiv(T * TOP_K + nt * N_EXPERTS * RUN_ALIGN, MOE_ROWS) + N_EXPERTS) * MOE_ROWS
    cnt = counts[:, 0, :N_EXPERTS].astype(jnp.int32)
    run_len = (cnt + RUN_ALIGN - 1) // RUN_ALIGN * RUN_ALIGN
    sizes = jnp.sum(run_len, axis=0)
    padded = (sizes + MOE_ROWS - 1) // MOE_ROWS * MOE_ROWS
    pends = jnp.cumsum(padded)
    pstart = pends - padded
    run_start = pstart[None, :] + jnp.cumsum(run_len, axis=0) - run_len
    stage_off = jnp.cumsum(run_len, axis=1) - run_len
    n_used = (pends[-1:] // MOE_ROWS).astype(jnp.int32)
    blk_row = jnp.arange(n_rows // MOE_ROWS, dtype=jnp.int32) * MOE_ROWS
    blk_e = jnp.minimum(jnp.sum(pends[None, :] <= blk_row[:, None], axis=1),
                        N_EXPERTS - 1).astype(jnp.int32)
    flat = lambda a: a.reshape(-1).astype(jnp.int32)

    xs = _dispatch(h2, srow, flat(run_start), flat(run_len), flat(stage_off),
                   flat(pstart + sizes), flat(padded - sizes), n_used, n_rows)
    y = _experts(xs, blk_e, n_used, p["wg"], p["wl"], p["bg"], p["bl"], p["wd"], p["bd"])
    out = _combine(y, x1, srow, gates, flat(run_start), flat(run_len), flat(stage_off),
                   p["norm_final"])
    return out.reshape(B, S, D_MODEL)


def kernel(x_prompt, x_sample, norm_mix, w_in, lambda_q1, lambda_k1, lambda_q2, lambda_k2, subln_w,
           lb_fwd, lb_bwd, hgrn_norm, w_out, norm_ffn, w_router, b_router, w_gate_up, b_gate_up,
           w_down, b_down, norm_final):
    l = 0
    lam = (jnp.exp(jnp.sum(lambda_q1[l] * lambda_k1[l])) - jnp.exp(jnp.sum(lambda_q2[l] * lambda_k2[l]))
           + LAM_INIT)
    wg, wl = _split_gate_up(w_gate_up[l])
    bgu = b_gate_up[l]
    p = {
        "norm_mix": norm_mix[l][None, :],
        "w_in": w_in[l].astype(BF16),
        "lam": lam.reshape(1).astype(F32),
        "subln_w": subln_w[l][None, :],
        "lb_fwd": jnp.cumsum(jax.nn.softmax(lb_fwd, axis=0), axis=0)[l][None, :],
        "lb_bwd": jnp.cumsum(jax.nn.softmax(lb_bwd, axis=0), axis=0)[l][None, :],
        "hgrn_norm": hgrn_norm[l][None, :],
        "w_out": w_out[l].astype(BF16),
        "norm_ffn": norm_ffn[l][None, :],
        "w_router": jnp.pad(w_router[l], ((0, 0), (0, LANES - N_EXPERTS))),
        "b_router": jnp.pad(b_router[l], (0, LANES - N_EXPERTS), constant_values=-jnp.inf)[None, :],
        "wg": wg,
        "wl": wl,
        "bg": bgu[:, None, 0::2],
        "bl": bgu[:, None, 1::2],
        "wd": w_down[l].astype(BF16),
        "bd": b_down[l][:, None, :],
        "norm_final": norm_final[None, :],
        "rope": _rope_tables(max(x_prompt.shape[1], x_sample.shape[1])),
    }
    return (_trunk(x_prompt, p), _trunk(x_sample, p))
```

```python
import math

import jax
import jax.numpy as jnp
from jax import lax
from jax.experimental import pallas as pl
from jax.experimental.pallas import tpu as pltpu

D_MODEL = 1024
HEAD_DIM = 64
ATT_HEADS = 4
ATT_V = 2 * HEAD_DIM
ATT_WIDTH = ATT_HEADS * ATT_V
ROT_DIM = HEAD_DIM // 4
ROPE_THETA = 500000.0
HG_HEADS = 4
HG_K = 128
HG_WIDTH = HG_HEADS * HG_K
IN_COLS = 4096
GROUP_COLS = 512
N_EXPERTS = 32
TOP_K = 4
D_FF = 1024
SWIGLU_LIMIT = 7.0
SWIGLU_ALPHA = 1.702
EPS = 1e-5
LAM_INIT = 0.8 - 0.6 * math.exp(-0.3 * 0)

LANES = 128
SUBLANES = 8
RUN_ALIGN = SUBLANES
STAGE_BLK = 256
assert STAGE_BLK <= 256
HG_CHUNK = 128
HG_BLOCK = 512
ATT_TK = 256
ATT_UNROLL = 16
LOG2_E = math.log2(math.e)
MOE_ROWS = 512
VMEM_LIMIT = 56 * 1024 * 1024

F32 = jnp.float32
BF16 = jnp.bfloat16
U32 = jnp.uint32
PACK_W = D_MODEL // 2
NT_DIMS = (((1,), (1,)), ((), ()))
TN_DIMS = (((0,), (0,)), ((), ()))


def _params(*sem):
    return pltpu.CompilerParams(dimension_semantics=sem, vmem_limit_bytes=VMEM_LIMIT)


def _inproj_kernel(x_ref, nw_ref, w_ref, cos_ref, sa_ref, sb_ref, lbf_ref, lbb_ref,
                   q_ref, k_ref, v_ref, qh_ref, lff_ref, lfb_ref, ih_ref, g_ref):
    x = x_ref[...]
    ms = jnp.mean(x * x, axis=-1, keepdims=True)
    h = (x * lax.rsqrt(ms + EPS) * nw_ref[...]).astype(BF16)

    def proj(c):
        return jnp.dot(h, w_ref[:, c * GROUP_COLS:(c + 1) * GROUP_COLS],
                       preferred_element_type=F32)

    cos, sa, sb = cos_ref[...], sa_ref[...], sb_ref[...]

    def rope(p):
        outs = []
        for j in range(GROUP_COLS // LANES):
            xx = p[:, j * LANES:(j + 1) * LANES]
            outs.append(xx * cos + pltpu.roll(xx, LANES - ROT_DIM // 2, 1) * sa
                        + pltpu.roll(xx, ROT_DIM // 2, 1) * sb)
        return jnp.concatenate(outs, axis=1)

    def log_forget(p, lb):
        return jnp.log(lb + (1.0 - lb) * jax.nn.sigmoid(p))

    q_ref[...] = (rope(proj(0)) * (HEAD_DIM ** -0.5 * LOG2_E)).astype(BF16)
    k_ref[...] = rope(proj(1)).astype(BF16)
    va = proj(2).astype(BF16)
    ones = jnp.ones((va.shape[0], ATT_V), BF16)
    for hh in range(ATT_HEADS):
        v_ref[:, 2 * hh * ATT_V:(2 * hh + 1) * ATT_V] = va[:, hh * ATT_V:(hh + 1) * ATT_V]
        v_ref[:, (2 * hh + 1) * ATT_V:(2 * hh + 2) * ATT_V] = ones
    qh_ref[...] = proj(3).astype(BF16)
    lff_ref[...] = log_forget(proj(4), lbf_ref[...])
    lfb_ref[...] = log_forget(proj(5), lbb_ref[...])
    ih_ref[...] = proj(6).astype(BF16)
    gh = proj(7)
    g_ref[...] = (gh * jax.nn.sigmoid(gh)).astype(BF16)


def _inproj(x2d, seq, norm_w, w_in, cos_t, sa_t, sb_t, lbf, lbb):
    T = x2d.shape[0]
    tm = min(1024, seq)
    nseq = seq // tm
    row = lambda i: (i, 0)
    fixed = lambda i: (0, 0)
    tab = lambda i: (i % nseq, 0)
    out_dt = [BF16, BF16, BF16, BF16, F32, F32, BF16, BF16]
    out_w = [GROUP_COLS, GROUP_COLS, 2 * ATT_WIDTH] + [GROUP_COLS] * 5
    return pl.pallas_call(
        _inproj_kernel,
        grid=(T // tm,),
        in_specs=[
            pl.BlockSpec((tm, D_MODEL), row),
            pl.BlockSpec((1, D_MODEL), fixed),
            pl.BlockSpec((D_MODEL, IN_COLS), fixed),
            pl.BlockSpec((tm, LANES), tab),
            pl.BlockSpec((tm, LANES), tab),
            pl.BlockSpec((tm, LANES), tab),
            pl.BlockSpec((1, GROUP_COLS), fixed),
            pl.BlockSpec((1, GROUP_COLS), fixed),
        ],
        out_specs=[pl.BlockSpec((tm, w), row) for w in out_w],
        out_shape=[jax.ShapeDtypeStruct((T, w), dt) for w, dt in zip(out_w, out_dt)],
        compiler_params=_params("parallel"),
        name="inproj",
    )(x2d, norm_w, w_in, cos_t, sa_t, sb_t, lbf, lbb)


def _attn_kernel(lam_ref, q_ref, k_ref, v_ref, w_ref, o_ref, m_s, acc_s):
    n_chunks = k_ref.shape[0] // ATT_TK
    q = q_ref[...]
    lane = lax.broadcasted_iota(jnp.int32, q.shape, 1)
    zero = jnp.zeros_like(q)
    qz = (jnp.where(lane < HEAD_DIM, q, zero), jnp.where(lane >= HEAD_DIM, q, zero))
    m_s[...] = jnp.full(m_s.shape, -jnp.inf, F32)
    acc_s[...] = jnp.zeros(acc_s.shape, F32)

    def chunk(j):
        start = pl.multiple_of(j * ATT_TK, ATT_TK)
        kc = k_ref[pl.ds(start, ATT_TK), :]
        vc = v_ref[pl.ds(start, ATT_TK), :]
        for c in range(2):
            s = lax.dot_general(qz[c], kc, NT_DIMS, preferred_element_type=F32)
            m_prev = m_s[c]
            m_next = jnp.maximum(m_prev, jnp.max(s, axis=1, keepdims=True))
            alpha = jnp.exp2(m_prev - m_next)
            p = jnp.exp2(s - jnp.tile(m_next, (1, ATT_TK // LANES))).astype(BF16)
            acc_s[c] = acc_s[c] * jnp.tile(alpha, (1, 2)) + jnp.dot(p, vc, preferred_element_type=F32)
            m_s[c] = m_next

    unroll = math.gcd(n_chunks, ATT_UNROLL)

    def body(jj, carry):
        for u in range(unroll):
            chunk(jj * unroll + u)
        return carry

    lax.fori_loop(0, n_chunks // unroll, body, 0)

    lam = lam_ref[0]
    a1, a2 = acc_s[0], acc_s[1]
    o = a1[:, :ATT_V] / a1[:, ATT_V:] - lam * (a2[:, :ATT_V] / a2[:, ATT_V:])
    ms = jnp.mean(o * o, axis=-1, keepdims=True)
    o = o * lax.rsqrt(ms + EPS) * w_ref[...] * (1.0 - LAM_INIT)
    o_ref[...] = o.astype(o_ref.dtype)


def _attention(q, k, v_ext, lam, subln_w):
    B, S, _ = q.shape
    tq = min(2048, S)
    return pl.pallas_call(
        _attn_kernel,
        grid=(B, ATT_HEADS, S // tq),
        in_specs=[
            pl.BlockSpec(memory_space=pltpu.SMEM),
            pl.BlockSpec((None, tq, ATT_V), lambda b, h, i: (b, i, h)),
            pl.BlockSpec((None, S, ATT_V), lambda b, h, i: (b, 0, h)),
            pl.BlockSpec((None, S, 2 * ATT_V), lambda b, h, i: (b, 0, h)),
            pl.BlockSpec((1, ATT_V), lambda b, h, i: (0, 0)),
        ],
        out_specs=pl.BlockSpec((None, tq, ATT_V), lambda b, h, i: (b, i, h)),
        out_shape=jax.ShapeDtypeStruct((B, S, ATT_WIDTH), BF16),
        scratch_shapes=[
            pltpu.VMEM((2, tq, LANES), F32),
            pltpu.VMEM((2, tq, 2 * ATT_V), F32),
        ],
        compiler_params=_params("parallel", "parallel", "arbitrary"),
        name="diff_attention",
    )(lam, q, k, v_ext, subln_w)


def _hgrn_kernel(qf_ref, lff_ref, vf_ref, qb_ref, lfb_ref, vb_ref, lvlf_ref, lvlb_ref,
                 of_ref, ob_ref, stf_ref, stb_ref):
    @pl.when(pl.program_id(1) == 0)
    def _():
        stf_ref[...] = jnp.zeros(stf_ref.shape, F32)
        stb_ref[...] = jnp.zeros(stb_ref.shape, F32)

    n_sub = qf_ref.shape[0] // HG_CHUNK
    for sub in range(n_sub):
        f_rows = pl.ds(sub * HG_CHUNK, HG_CHUNK)
        b_rows = pl.ds((n_sub - 1 - sub) * HG_CHUNK, HG_CHUNK)
        _hgrn_chunk(qf_ref.at[f_rows], lff_ref.at[f_rows], vf_ref.at[f_rows], lvlf_ref,
                    of_ref.at[f_rows], stf_ref, reverse=False)
        _hgrn_chunk(qb_ref.at[b_rows], lfb_ref.at[b_rows], vb_ref.at[b_rows], lvlb_ref,
                    ob_ref.at[b_rows], stb_ref, reverse=True)


def _hgrn_chunk(q_ref, lf_ref, v_ref, lvl_ref, o_ref, st_ref, *, reverse):
    C = q_ref.shape[0]

    lf = lf_ref[...]
    q = q_ref[...].astype(F32)
    v = v_ref[...]
    kk = 1.0 - jnp.exp(lf)
    row = lax.broadcasted_iota(jnp.int32, lf.shape, 0)
    hi_side, lo_side = (kk, q) if reverse else (q, kk)

    def halves(x, m):
        lo = [x[b:b + m] for b in range(0, C, 2 * m)]
        up = [x[b + m:b + 2 * m] for b in range(0, C, 2 * m)]
        return jnp.concatenate(lo, axis=0), jnp.concatenate(up, axis=0)

    def merge(lo, up, m):
        parts = []
        for b in range(0, C // 2, m):
            parts += [lo[b:b + m], up[b:b + m]]
        return jnp.concatenate(parts, axis=0)

    pref, tot = lf, lf
    xs = []
    m = 1
    while m < SUBLANES:
        upper = (row & m) != 0
        if reverse:
            g = jnp.where(upper, pref - lf, tot - pref + lf)
        else:
            g = jnp.where(upper, pref, tot - pref)
        xs.append((jnp.where(upper, hi_side, lo_side) * jnp.exp(g)).astype(BF16))
        grouped = tot.reshape(C // SUBLANES, SUBLANES, tot.shape[1])
        below = pltpu.roll(grouped, m, 1).reshape(tot.shape)
        above = pltpu.roll(grouped, SUBLANES - m, 1).reshape(tot.shape)
        pref = jnp.where(upper, pref + below, pref)
        tot = tot + jnp.where(upper, below, above)
        m *= 2
    while m < C:
        p_lo, p_up = halves(pref, m)
        t_lo, t_up = halves(tot, m)
        hi_up = halves(hi_side, m)[1]
        lo_lo = halves(lo_side, m)[0]
        if reverse:
            lf_lo, lf_up = halves(lf, m)
            g_up, g_lo = p_up - lf_up, t_lo - p_lo + lf_lo
        else:
            g_up, g_lo = p_up, t_lo - p_lo
        xs.append(merge(lo_lo * jnp.exp(g_lo), hi_up * jnp.exp(g_up), m).astype(BF16))
        t_new = t_lo + t_up
        pref = merge(p_lo, p_up + t_lo, m)
        tot = merge(t_new, t_new, m)
        m *= 2

    if reverse:
        gq, gk = tot - pref + lf, pref - lf
    else:
        gq, gk = pref, tot - pref
    xq = (q * jnp.exp(gq)).astype(BF16)
    xk = (kk * jnp.exp(gk)).astype(BF16)
    chunk_decay = jnp.exp(tot[0:1, :])
    qk = q * kk
    lvl = lvl_ref[...]
    at_level = [lvl == li + 1 for li in range(len(xs))]

    for h in range(HG_HEADS):
        sl = slice(h * HG_K, (h + 1) * HG_K)
        a = jnp.zeros((C, C), F32)
        for li, x in enumerate(xs):
            xh = x[:, sl]
            am = lax.dot_general(xh, xh, NT_DIMS, preferred_element_type=F32)
            a = jnp.where(at_level[li], am, a)
        vh = v[:, sl]
        st = st_ref[h]
        o = jnp.dot(a.astype(BF16), vh, preferred_element_type=F32)
        o += lax.dot_general(xq[:, sl], st.astype(BF16), NT_DIMS, preferred_element_type=F32)
        o += jnp.sum(qk[:, sl], axis=1, keepdims=True) * vh.astype(F32)
        o_ref[:, sl] = o
        st_ref[h] = st * chunk_decay[:, sl] + lax.dot_general(
            vh, xk[:, sl], TN_DIMS, preferred_element_type=F32)


def _level_table(reverse):
    t = jnp.arange(HG_CHUNK, dtype=jnp.int32)[:, None]
    s = jnp.arange(HG_CHUNK, dtype=jnp.int32)[None, :]
    x = t ^ s
    lvl = jnp.zeros_like(x)
    for b in range(HG_CHUNK.bit_length() - 1):
        lvl = jnp.where(x >= (1 << b), b + 1, lvl)
    active = (s > t) if reverse else (s < t)
    return jnp.where(active, lvl, 0)


def _hgrn(qh, lf_f, lf_b, vh):
    B, S, _ = qh.shape
    rows = math.gcd(S, HG_BLOCK)
    nc = S // rows
    fwd = pl.BlockSpec((None, rows, HG_WIDTH), lambda b, c: (b, c, 0))
    bwd = pl.BlockSpec((None, rows, HG_WIDTH), lambda b, c: (b, nc - 1 - c, 0))
    lvl = pl.BlockSpec((HG_CHUNK, HG_CHUNK), lambda b, c: (0, 0))
    out = jax.ShapeDtypeStruct((B, S, HG_WIDTH), F32)
    state = pltpu.VMEM((HG_HEADS, HG_K, HG_K), F32)
    return pl.pallas_call(
        _hgrn_kernel,
        grid=(B, nc),
        in_specs=[fwd, fwd, fwd, bwd, bwd, bwd, lvl, lvl],
        out_specs=[fwd, bwd],
        out_shape=[out, out],
        scratch_shapes=[state, state],
        compiler_params=_params("parallel", "arbitrary"),
        name="hgrn",
    )(qh, lf_f, vh, qh, lf_b, vh, _level_table(False), _level_table(True))


def _outproj_kernel(x_ref, oa_ref, of_ref, ob_ref, g_ref, hn_ref, wo_ref, fn_ref, wr_ref, br_ref,
                    x1_ref, h2_ref, srow_ref, gate_ref, cnt_ref):
    tm = x_ref.shape[0]

    oh = of_ref[...] + ob_ref[...]
    ms = jnp.mean(oh * oh, axis=-1, keepdims=True)
    ohn = oh * lax.rsqrt(ms + EPS) * hn_ref[...] * g_ref[...].astype(F32)
    mix = jnp.dot(oa_ref[...], wo_ref[:ATT_WIDTH, :], preferred_element_type=F32)
    mix += jnp.dot(ohn.astype(BF16), wo_ref[ATT_WIDTH:, :], preferred_element_type=F32)
    x1 = x_ref[...] + mix
    x1_ref[...] = x1
    ms = jnp.mean(x1 * x1, axis=-1, keepdims=True)
    h2 = x1 * lax.rsqrt(ms + EPS) * fn_ref[...]
    h2_ref[...] = h2.astype(BF16)

    wr = wr_ref[...]
    w_hi = wr.astype(BF16)
    w_lo = (wr - w_hi.astype(F32)).astype(BF16)
    h_hi = h2.astype(BF16)
    h_lo = (h2 - h_hi.astype(F32)).astype(BF16)
    logits = (jnp.dot(h_hi, w_hi, preferred_element_type=F32)
              + jnp.dot(h_hi, w_lo, preferred_element_type=F32)
              + jnp.dot(h_lo, w_hi, preferred_element_type=F32)) + br_ref[...]
    lane = lax.broadcasted_iota(jnp.int32, logits.shape, 1)
    lane_f = lane.astype(F32)
    vals, idxs = [], []
    for _ in range(TOP_K):
        mx = jnp.max(logits, axis=1, keepdims=True)
        ix = jnp.min(jnp.where(logits == mx, lane_f, float(LANES)), axis=1, keepdims=True)
        vals.append(mx)
        idxs.append(ix)
        logits = jnp.where(lane_f == ix, -jnp.inf, logits)
    es = [jnp.exp(vv - vals[0]) for vv in vals]
    den = es[0] + es[1] + es[2] + es[3]

    onehots = [(lane_f == ix).astype(F32) for ix in idxs]
    multi = onehots[0] + onehots[1] + onehots[2] + onehots[3]
    r_i = lax.broadcasted_iota(jnp.int32, (tm, tm), 0)
    c_i = lax.broadcasted_iota(jnp.int32, (tm, tm), 1)
    tri = jnp.where(c_i < r_i, 1.0, 0.0).astype(BF16)
    before = jnp.dot(tri, multi.astype(BF16), preferred_element_type=F32)

    cnt = jnp.broadcast_to(jnp.sum(multi, axis=0, keepdims=True), cnt_ref.shape)
    run_len = jnp.ceil(cnt * (1.0 / RUN_ALIGN)) * RUN_ALIGN
    e_i = lax.broadcasted_iota(jnp.int32, (LANES, LANES), 0)
    e_j = lax.broadcasted_iota(jnp.int32, (LANES, LANES), 1)
    lower = jnp.where(e_i < e_j, 1.0, 0.0)
    stage_off = jnp.dot(run_len, lower, preferred_element_type=F32,
                        precision=lax.Precision.HIGHEST)[0:1, :]
    srow = jnp.full(logits.shape, -1.0, F32)
    gates = jnp.zeros(logits.shape, F32)
    for kslot in range(TOP_K):
        here = lane == kslot
        gates = jnp.where(here, es[kslot] / den, gates)
        row_k = jnp.sum(onehots[kslot] * (before + stage_off), axis=1, keepdims=True)
        srow = jnp.where(here, row_k, srow)
    srow_ref[...] = srow
    gate_ref[...] = gates
    cnt_ref[...] = cnt


def _outproj(x2d, oa, o_f, o_b, g, hgrn_norm, w_out, norm_ffn, w_router, b_router):
    T = x2d.shape[0]
    tm = _moe_tile(T)
    row = lambda i: (i, 0)
    fixed = lambda i: (0, 0)
    return pl.pallas_call(
        _outproj_kernel,
        grid=(T // tm,),
        in_specs=[
            pl.BlockSpec((tm, D_MODEL), row),
            pl.BlockSpec((tm, ATT_WIDTH), row),
            pl.BlockSpec((tm, HG_WIDTH), row),
            pl.BlockSpec((tm, HG_WIDTH), row),
            pl.BlockSpec((tm, HG_WIDTH), row),
            pl.BlockSpec((1, HG_WIDTH), fixed),
            pl.BlockSpec((D_MODEL, D_MODEL), fixed),
            pl.BlockSpec((1, D_MODEL), fixed),
            pl.BlockSpec((D_MODEL, LANES), fixed),
            pl.BlockSpec((1, LANES), fixed),
        ],
        out_specs=[
            pl.BlockSpec((tm, D_MODEL), row),
            pl.BlockSpec((tm, D_MODEL), row),
            pl.BlockSpec((tm, LANES), row),
            pl.BlockSpec((tm, LANES), row),
            pl.BlockSpec((None, SUBLANES, LANES), lambda i: (i, 0, 0)),
        ],
        out_shape=[
            jax.ShapeDtypeStruct((T, D_MODEL), F32),
            jax.ShapeDtypeStruct((T, D_MODEL), BF16),
            jax.ShapeDtypeStruct((T, LANES), F32),
            jax.ShapeDtypeStruct((T, LANES), F32),
            jax.ShapeDtypeStruct((T // tm, SUBLANES, LANES), F32),
        ],
        compiler_params=_params("parallel"),
        name="outproj_router",
    )(x2d, oa, o_f, o_b, g, hgrn_norm, w_out, norm_ffn, w_router, b_router)


def _moe_tile(T):
    return min(512, T)


def _pack_pairs(x):
    hi = lax.bitcast_convert_type(x[:, :PACK_W], U32)
    lo = lax.bitcast_convert_type(x[:, PACK_W:], U32)
    return (hi & jnp.uint32(0xFFFF0000)) | (lo >> 16)


def _unpack_pairs(u):
    hi = lax.bitcast_convert_type(u & jnp.uint32(0xFFFF0000), F32).astype(BF16)
    lo = lax.bitcast_convert_type(u << 16, F32).astype(BF16)
    return jnp.concatenate([hi, lo], axis=1)


def _stage_rows(tm):
    return pl.cdiv(TOP_K * tm + N_EXPERTS * RUN_ALIGN, STAGE_BLK) * STAGE_BLK


def _run_dmas(tile, run_start_ref, run_len_ref, stage_off_ref, make_copy, act):
    def per_expert(e, carry):
        n = pl.multiple_of(run_len_ref[tile * N_EXPERTS + e], RUN_ALIGN)
        src = pl.multiple_of(stage_off_ref[tile * N_EXPERTS + e], RUN_ALIGN)
        dst = pl.multiple_of(run_start_ref[tile * N_EXPERTS + e], RUN_ALIGN)

        @pl.when(n > 0)
        def _():
            act(make_copy(src, dst, n))

        return carry

    lax.fori_loop(0, N_EXPERTS, per_expert, 0)


def _dispatch_kernel(run_start_ref, run_len_ref, stage_off_ref, pad_start_ref, pad_len_ref, nu_ref,
                     h2_ref, srow_ref, xs_ref, stage, zero_s, sems, zsem):
    i = pl.program_id(0)
    nt = pl.num_programs(0)
    slot = i % 2
    tm = h2_ref.shape[0]
    sr = stage.shape[1]

    def copy_out(s):
        def make(stage_row, hbm_row, size):
            return pltpu.make_async_copy(stage.at[s, pl.ds(stage_row, size)],
                                         xs_ref.at[pl.ds(hbm_row, size)], sems.at[s])
        return make

    tables = (run_start_ref, run_len_ref, stage_off_ref)

    @pl.when(i >= 2)
    def _():
        _run_dmas(i - 2, *tables, copy_out(slot), lambda c: c.wait())

    @pl.when(i == 0)
    def _():
        zero_s[...] = jnp.zeros(zero_s.shape, U32)

        def zero_copy(hbm_row, size):
            return pltpu.make_async_copy(zero_s.at[pl.ds(0, size)], xs_ref.at[pl.ds(hbm_row, size)], zsem)

        def pads(act):
            def per_expert(e, c):
                n = pl.multiple_of(pad_len_ref[e], RUN_ALIGN)

                @pl.when(n > 0)
                def _():
                    act(zero_copy(pl.multiple_of(pad_start_ref[e], RUN_ALIGN), n))

                return c

            lax.fori_loop(0, N_EXPERTS, per_expert, 0)

        def tail(act):
            def per_block(j, c):
                act(zero_copy(pl.multiple_of(j * MOE_ROWS, MOE_ROWS), MOE_ROWS))
                return c

            lax.fori_loop(nu_ref[0], xs_ref.shape[0] // MOE_ROWS, per_block, 0)

        pads(lambda c: c.start())
        tail(lambda c: c.start())
        pads(lambda c: c.wait())
        tail(lambda c: c.wait())

    srow_t = srow_ref[...].T
    h2 = h2_ref[...]
    r_iota = lax.broadcasted_iota(jnp.int32, (STAGE_BLK, tm), 0).astype(F32).astype(BF16)
    one, zero = jnp.ones((STAGE_BLK, tm), BF16), jnp.zeros((STAGE_BLK, tm), BF16)
    for b in range(sr // STAGE_BLK):
        rel = [(srow_t[kslot:kslot + 1, :] - b * STAGE_BLK).astype(BF16) for kslot in range(TOP_K)]
        hit = r_iota == rel[0]
        for kslot in range(1, TOP_K):
            hit = hit | (r_iota == rel[kslot])
        onehot = jnp.where(hit, one, zero)
        stage[slot, b * STAGE_BLK:(b + 1) * STAGE_BLK, :] = _pack_pairs(
            jnp.dot(onehot, h2, preferred_element_type=F32))
    _run_dmas(i, *tables, copy_out(slot), lambda c: c.start())

    @pl.when(i == nt - 1)
    def _():
        @pl.when(nt >= 2)
        def _():
            _run_dmas(i - 1, *tables, copy_out(1 - slot), lambda c: c.wait())

        _run_dmas(i, *tables, copy_out(slot), lambda c: c.wait())


def _dispatch(h2, srow, run_start, run_len, stage_off, pad_start, pad_len, n_used, n_rows):
    T = h2.shape[0]
    tm = _moe_tile(T)
    row = lambda i, *_: (i, 0)
    return pl.pallas_call(
        _dispatch_kernel,
        grid_spec=pltpu.PrefetchScalarGridSpec(
            num_scalar_prefetch=6,
            grid=(T // tm,),
            in_specs=[
                pl.BlockSpec((tm, D_MODEL), row),
                pl.BlockSpec((tm, LANES), row),
            ],
            out_specs=pl.BlockSpec(memory_space=pl.ANY),
            scratch_shapes=[
                pltpu.VMEM((2, _stage_rows(tm), PACK_W), U32),
                pltpu.VMEM((MOE_ROWS, PACK_W), U32),
                pltpu.SemaphoreType.DMA((2,)),
                pltpu.SemaphoreType.DMA(()),
            ],
        ),
        out_shape=jax.ShapeDtypeStruct((n_rows, PACK_W), U32),
        compiler_params=_params("arbitrary"),
        name="moe_dispatch",
    )(run_start, run_len, stage_off, pad_start, pad_len, n_used, h2, srow)


def _expert_kernel(be_ref, nu_ref, xs_ref, wg_ref, wl_ref, bg_ref, bl_ref, wd_ref, bd_ref, y_ref):
    @pl.when(pl.program_id(0) < nu_ref[0])
    def _():
        x = _unpack_pairs(xs_ref[...])
        glu = jnp.dot(x, wg_ref[...], preferred_element_type=F32) + bg_ref[...]
        lin = jnp.dot(x, wl_ref[...], preferred_element_type=F32) + bl_ref[...]
        glu = jnp.minimum(glu, SWIGLU_LIMIT)
        lin = jnp.clip(lin, -SWIGLU_LIMIT, SWIGLU_LIMIT)
        act = glu * jax.nn.sigmoid(SWIGLU_ALPHA * glu) * (lin + 1.0)
        y = jnp.dot(act.astype(BF16), wd_ref[...], preferred_element_type=F32) + bd_ref[...]
        y_ref[...] = _pack_pairs(y.astype(BF16).astype(F32))

    @pl.when(pl.program_id(0) >= nu_ref[0])
    def _():
        y_ref[...] = jnp.zeros(y_ref.shape, U32)


def _experts(xs, blk_e, n_used, wg, wl, bg, bl, wd, bd):
    n_rows = xs.shape[0]
    nblk = n_rows // MOE_ROWS
    rows = lambda i, be, nu: (jnp.minimum(i, nu[0] - 1), 0)
    per_e = lambda i, be, nu: (be[jnp.minimum(i, nu[0] - 1)], 0, 0)
    return pl.pallas_call(
        _expert_kernel,
        grid_spec=pltpu.PrefetchScalarGridSpec(
            num_scalar_prefetch=2,
            grid=(nblk,),
            in_specs=[
                pl.BlockSpec((MOE_ROWS, PACK_W), rows),
                pl.BlockSpec((None, D_MODEL, D_FF), per_e),
                pl.BlockSpec((None, D_MODEL, D_FF), per_e),
                pl.BlockSpec((None, 1, D_FF), per_e),
                pl.BlockSpec((None, 1, D_FF), per_e),
                pl.BlockSpec((None, D_FF, D_MODEL), per_e),
                pl.BlockSpec((None, 1, D_MODEL), per_e),
            ],
            out_specs=pl.BlockSpec((MOE_ROWS, PACK_W), lambda i, be, nu: (i, 0)),
        ),
        out_shape=jax.ShapeDtypeStruct((n_rows, PACK_W), U32),
        compiler_params=_params("arbitrary"),
        name="moe_experts",
    )(blk_e, n_used, xs, wg, wl, bg, bl, wd, bd)


def _split_kernel(w_ref, perm_ref, g_ref, l_ref):
    grp = 2 * LANES
    for j in range(w_ref.shape[1] // grp):
        w = w_ref[:, j * grp:(j + 1) * grp].astype(BF16)
        r = jnp.dot(w, perm_ref[...], preferred_element_type=F32)
        g_ref[:, j * LANES:(j + 1) * LANES] = r[:, :LANES].astype(BF16)
        l_ref[:, j * LANES:(j + 1) * LANES] = r[:, LANES:].astype(BF16)


def _split_gate_up(w_gate_up):
    E, D, F2 = w_gate_up.shape
    grp = 2 * LANES
    src = jnp.arange(grp, dtype=jnp.int32)[:, None]
    dst = jnp.arange(grp, dtype=jnp.int32)[None, :]
    perm = (dst == (src // 2 + LANES * (src % 2))).astype(BF16)
    half = jax.ShapeDtypeStruct((E, D, F2 // 2), BF16)
    out_spec = pl.BlockSpec((None, D, F2 // 2), lambda e: (e, 0, 0))
    return pl.pallas_call(
        _split_kernel,
        grid=(E,),
        in_specs=[pl.BlockSpec((None, D, F2), lambda e: (e, 0, 0)),
                  pl.BlockSpec((grp, grp), lambda e: (0, 0))],
        out_specs=[out_spec, out_spec],
        out_shape=[half, half],
        compiler_params=_params("parallel"),
        name="split_gate_up",
    )(w_gate_up, perm)


def _combine_kernel(run_start_ref, run_len_ref, stage_off_ref,
                    y_ref, x1_ref, srow_ref, gate_ref, nf_ref, o_ref, stage, sems):
    i = pl.program_id(0)
    nt = pl.num_programs(0)
    slot = i % 2
    tm = x1_ref.shape[0]
    sr = stage.shape[1]

    def copy_in(s):
        def make(stage_row, hbm_row, size):
            return pltpu.make_async_copy(y_ref.at[pl.ds(hbm_row, size)],
                                         stage.at[s, pl.ds(stage_row, size)], sems.at[s])
        return make

    tables = (run_start_ref, run_len_ref, stage_off_ref)

    @pl.when(i == 0)
    def _():
        stage[...] = jnp.zeros(stage.shape, U32)
        _run_dmas(i, *tables, copy_in(0), lambda c: c.start())

    @pl.when(i + 1 < nt)
    def _():
        _run_dmas(i + 1, *tables, copy_in(1 - slot), lambda c: c.start())

    _run_dmas(i, *tables, copy_in(slot), lambda c: c.wait())

    srow = srow_ref[...]
    gates = gate_ref[...]
    lane = lax.broadcasted_iota(jnp.int32, gates.shape, 1)
    pick = lambda a, kslot: jnp.sum(jnp.where(lane == kslot, a, 0.0), axis=1, keepdims=True)
    cols = [pick(srow, kslot) for kslot in range(TOP_K)]
    g_cols = [jnp.broadcast_to(pick(gates, kslot).astype(BF16), (tm, STAGE_BLK))
              for kslot in range(TOP_K)]
    r_iota = lax.broadcasted_iota(jnp.int32, (tm, STAGE_BLK), 1).astype(F32).astype(BF16)
    blocks = []
    for b in range(sr // STAGE_BLK):
        w = jnp.zeros((tm, STAGE_BLK), BF16)
        for c, g_k in zip(cols, g_cols):
            w = jnp.where(r_iota == (c - b * STAGE_BLK).astype(BF16), g_k, w)
        blocks.append(w)
    weights = jnp.concatenate(blocks, axis=1)
    staged = _unpack_pairs(stage[slot])
    acc = x1_ref[...] + jnp.dot(weights, staged, preferred_element_type=F32)
    ms = jnp.mean(acc * acc, axis=-1, keepdims=True)
    o_ref[...] = acc * lax.rsqrt(ms + EPS) * nf_ref[...]


def _combine(y, x1, srow, gates, run_start, run_len, stage_off, norm_final):
    T = x1.shape[0]
    tm = _moe_tile(T)
    row = lambda i, *_: (i, 0)
    return pl.pallas_call(
        _combine_kernel,
        grid_spec=pltpu.PrefetchScalarGridSpec(
            num_scalar_prefetch=3,
            grid=(T // tm,),
            in_specs=[
                pl.BlockSpec(memory_space=pl.ANY),
                pl.BlockSpec((tm, D_MODEL), row),
                pl.BlockSpec((tm, LANES), row),
                pl.BlockSpec((tm, LANES), row),
                pl.BlockSpec((1, D_MODEL), lambda i, *_: (0, 0)),
            ],
            out_specs=pl.BlockSpec((tm, D_MODEL), row),
            scratch_shapes=[
                pltpu.VMEM((2, _stage_rows(tm), PACK_W), U32),
                pltpu.SemaphoreType.DMA((2,)),
            ],
        ),
        out_shape=jax.ShapeDtypeStruct((T, D_MODEL), F32),
        compiler_params=_params("arbitrary"),
        name="moe_combine",
    )(run_start, run_len, stage_off, y, x1, srow, gates, norm_final)


def _rope_tables(seq):
    half = ROT_DIM // 2
    inv = ROPE_THETA ** (-jnp.arange(0, ROT_DIM, 2, dtype=F32) / ROT_DIM)
    ang = jnp.arange(seq, dtype=F32)[:, None] * inv[None, :]
    cos, sin = jnp.cos(ang), jnp.sin(ang)
    pad = jnp.zeros((seq, HEAD_DIM - ROT_DIM), F32)
    zero = jnp.zeros((seq, half), F32)
    cos_t = jnp.concatenate([cos, cos, pad + 1.0], axis=1)
    sa_t = jnp.concatenate([-sin, zero, pad], axis=1)
    sb_t = jnp.concatenate([zero, sin, pad], axis=1)
    rep = LANES // HEAD_DIM
    return tuple(jnp.tile(t, (1, rep)) for t in (cos_t, sa_t, sb_t))


def _trunk(x, p):
    B, S, _ = x.shape
    T = B * S
    x2d = x.reshape(T, D_MODEL)
    q, k, v, qh, lff, lfb, ih, g = _inproj(x2d, S, p["norm_mix"], p["w_in"],
                                           *(t[:S] for t in p["rope"]), p["lb_fwd"], p["lb_bwd"])
    to3 = lambda a: a.reshape(B, S, a.shape[-1])
    oa = _attention(to3(q), to3(k), to3(v), p["lam"], p["subln_w"])
    o_f, o_b = _hgrn(to3(qh), to3(lff), to3(lfb), to3(ih))
    x1, h2, srow, gates, counts = _outproj(
        x2d, oa.reshape(T, ATT_WIDTH), o_f.reshape(T, HG_WIDTH), o_b.reshape(T, HG_WIDTH), g,
        p["hgrn_norm"], p["w_out"], p["norm_ffn"], p["w_router"], p["b_router"])

    tm = _moe_tile(T)
    nt = T // tm
    n_rows = (pl.cdiv(T * TOP_K + nt * N_EXPERTS * RUN_ALIGN, MOE_ROWS) + N_EXPERTS) * MOE_ROWS
    cnt = counts[:, 0, :N_EXPERTS].astype(jnp.int32)
    run_len = (cnt + RUN_ALIGN - 1) // RUN_ALIGN * RUN_ALIGN
    sizes = jnp.sum(run_len, axis=0)
    padded = (sizes + MOE_ROWS - 1) // MOE_ROWS * MOE_ROWS
    pends = jnp.cumsum(padded)
    pstart = pends - padded
    run_start = pstart[None, :] + jnp.cumsum(run_len, axis=0) - run_len
    stage_off = jnp.cumsum(run_len, axis=1) - run_len
    n_used = (pends[-1:] // MOE_ROWS).astype(jnp.int32)
    blk_row = jnp.arange(n_rows // MOE_ROWS, dtype=jnp.int32) * MOE_ROWS
    blk_e = jnp.minimum(jnp.sum(pends[None, :] <= blk_row[:, None], axis=1),
                        N_EXPERTS - 1).astype(jnp.int32)
    flat = lambda a: a.reshape(-1).astype(jnp.int32)

    xs = _dispatch(h2, srow, flat(run_start), flat(run_len), flat(stage_off),
                   flat(pstart + sizes), flat(padded - sizes), n_used, n_rows)
    y = _experts(xs, blk_e, n_used, p["wg"], p["wl"], p["bg"], p["bl"], p["wd"], p["bd"])
    out = _combine(y, x1, srow, gates, flat(run_start), flat(run_len), flat(stage_off),
                   p["norm_final"])
    return out.reshape(B, S, D_MODEL)


def kernel(x_prompt, x_sample, norm_mix, w_in, lambda_q1, lambda_k1, lambda_q2, lambda_k2, subln_w,
           lb_fwd, lb_bwd, hgrn_norm, w_out, norm_ffn, w_router, b_router, w_gate_up, b_gate_up,
           w_down, b_down, norm_final):
    l = 0
    lam = (jnp.exp(jnp.sum(lambda_q1[l] * lambda_k1[l])) - jnp.exp(jnp.sum(lambda_q2[l] * lambda_k2[l]))
           + LAM_INIT)
    wg, wl = _split_gate_up(w_gate_up[l])
    bgu = b_gate_up[l]
    p = {
        "norm_mix": norm_mix[l][None, :],
        "w_in": w_in[l].astype(BF16),
        "lam": lam.reshape(1).astype(F32),
        "subln_w": subln_w[l][None, :],
        "lb_fwd": jnp.cumsum(jax.nn.softmax(lb_fwd, axis=0), axis=0)[l][None, :],
        "lb_bwd": jnp.cumsum(jax.nn.softmax(lb_bwd, axis=0), axis=0)[l][None, :],
        "hgrn_norm": hgrn_norm[l][None, :],
        "w_out": w_out[l].astype(BF16),
        "norm_ffn": norm_ffn[l][None, :],
        "w_router": jnp.pad(w_router[l], ((0, 0), (0, LANES - N_EXPERTS))),
        "b_router": jnp.pad(b_router[l], (0, LANES - N_EXPERTS), constant_values=-jnp.inf)[None, :],
        "wg": wg,
        "wl": wl,
        "bg": bgu[:, None, 0::2],
        "bl": bgu[:, None, 1::2],
        "wd": w_down[l].astype(BF16),
        "bd": b_down[l][:, None, :],
        "norm_final": norm_final[None, :],
        "rope": _rope_tables(max(x_prompt.shape[1], x_sample.shape[1])),
    }
    return (_trunk(x_prompt, p), _trunk(x_sample, p))
```
